```python
import math
import jax, jax.numpy as jnp
from jax import lax
import numpy as np

D_MODEL = 1024
BATCH = 4
SEQ = 4096
DEPTH = 1

D_RWKV = D_MODEL
RWKV_HEAD = 64
RWKV_HEADS = D_RWKV // RWKV_HEAD
LORA_W = 64
LORA_A = 64
LORA_G = 128
DECAY_SCALE = math.exp(-0.5)
GN_EPS = 64e-5
D_CONV = D_MODEL
CONV_WIDTH = 31
N_EXPERTS = 64
TOP_K = 8
N_GROUPS = 8
TOPK_GROUPS = 4
D_EXPERT = 256
D_SHARED = 256
ROUTED_SCALE = 2.5
MOE_BLOCK = 1024
N_BRANCH = 2
LN_EPS = 1e-5
RW_SPLITS = (D_RWKV, D_RWKV, D_RWKV, LORA_W, LORA_W, LORA_A, LORA_A, LORA_G)
RW_COLS = sum(RW_SPLITS)
P_TOTAL = RW_COLS + 2 * D_CONV + N_BRANCH * D_MODEL

kernel_name = "hybrid_rwkv7_conformer_moe_encoder"


def layer_norm(x, w=None, b=None, eps=LN_EPS):
    xf = x.astype(jnp.float32)
    mu = jnp.mean(xf, -1, keepdims=True)
    var = jnp.mean(jnp.square(xf - mu), -1, keepdims=True)
    y = (xf - mu) * lax.rsqrt(var + eps)
    if w is not None:
        y = y * w.astype(jnp.float32) + b.astype(jnp.float32)
    return y.astype(x.dtype)


def centred_shift(p, mu_prev, mu_next):
    p_prev = jnp.pad(p, ((0, 0), (1, 0), (0, 0)))[:, :-1]
    p_next = jnp.pad(p, ((0, 0), (0, 1), (0, 0)))[:, 1:]
    return p + mu_prev * (p_prev - p) + mu_next * (p_next - p)


def wkv7_scan(r, w, k, v, kk, a, reverse):
    B, S, H, N = r.shape
    xs = tuple(jnp.swapaxes(t, 0, 1) for t in (r, w, k, v, kk, a))

    def step(state, inp):
        r_t, w_t, k_t, v_t, kk_t, a_t = inp
        s_kk = jnp.einsum('bhvk,bhk->bhv', state, kk_t)
        new = (state * w_t[:, :, None, :]
               - s_kk[..., None] * (kk_t * a_t)[:, :, None, :]
               + v_t[..., None] * k_t[:, :, None, :])
        read = state if reverse else new
        return new, jnp.einsum('bhvk,bhk->bhv', read, r_t)

    _, ys = lax.scan(step, jnp.zeros((B, H, N, N), jnp.float32), xs, reverse=reverse)
    return jnp.swapaxes(ys, 0, 1)


def rwkv7_branch(p_rw, w0, w_up, a0, a_up, g_up, k_k, k_a, r_k, lnx_w, lnx_b):
    B, S, _ = p_rw.shape
    H, N = RWKV_HEADS, RWKV_HEAD
    f32 = jnp.float32
    idx = np.cumsum(RW_SPLITS)[:-1].tolist()
    r, k, v, wl_f, wl_b, al_f, al_b, gl = jnp.split(p_rw.astype(f32), idx, axis=-1)
    heads = lambda t: t.reshape(B, S, H, N)
    g = jax.nn.sigmoid(gl) @ g_up.astype(f32)
    kk = heads(k * k_k.astype(f32))
    kk = kk / jnp.maximum(jnp.linalg.norm(kk, axis=-1, keepdims=True), 1e-12)
    r_h, v_h = heads(r), heads(v)
    outs = []
    ks = []
    for d, (wl, al) in enumerate(((wl_f, al_f), (wl_b, al_b))):
        decay = jnp.exp(-DECAY_SCALE * jax.nn.sigmoid(w0[d].astype(f32) + jnp.tanh(wl) @ w_up[d].astype(f32)))
        a = jax.nn.sigmoid(a0[d].astype(f32) + al @ a_up[d].astype(f32))
        k_d = k * (1.0 + (a - 1.0) * k_a.astype(f32))
        ks.append(k_d)
        outs.append(wkv7_scan(r_h, heads(decay), heads(k_d), v_h, kk, heads(a), reverse=(d == 1)))
    y = outs[0] + outs[1]
    mu = jnp.mean(y, -1, keepdims=True)
    var = jnp.mean(jnp.square(y - mu), -1, keepdims=True)
    y = ((y - mu) * lax.rsqrt(var + GN_EPS)).reshape(B, S, D_RWKV) * lnx_w.astype(f32) + lnx_b.astype(f32)
    bonus = jnp.sum(r_h * heads(ks[0]) * r_k.astype(f32), -1, keepdims=True) * v_h
    return (y + bonus.reshape(B, S, D_RWKV)) * g


def conformer_conv_branch(p_conv, conv_w, conv_b, cnorm_w, cnorm_b, cpw_w, cpw_b):
    h_a, h_b = jnp.split(p_conv, 2, axis=-1)
    h = h_a * jax.nn.sigmoid(h_b)
    pad = CONV_WIDTH // 2
    h = lax.conv_general_dilated(h, conv_w[:, None, :].astype(h.dtype), window_strides=(1,),
                                 padding=[(pad, pad)], dimension_numbers=('NWC', 'WIO', 'NWC'),
                                 feature_group_count=D_CONV) + conv_b
    h = jax.nn.silu(layer_norm(h, cnorm_w, cnorm_b))
    return h @ cpw_w + cpw_b


def token_mix(u, w_in, shift_mu, w0, w_up, a0, a_up, g_up, k_k, k_a, r_k, lnx_w, lnx_b,
              conv_w, conv_b, cnorm_w, cnorm_b, cpw_w, cpw_b, gate_b, w_o):
    proj = u @ w_in
    p_rw, p_conv, p_gate = jnp.split(proj, [RW_COLS, RW_COLS + 2 * D_CONV], axis=-1)
    p_rw = centred_shift(p_rw, shift_mu[0], shift_mu[1])
    y_a = rwkv7_branch(p_rw, w0, w_up, a0, a_up, g_up, k_k, k_a, r_k, lnx_w, lnx_b).astype(u.dtype)
    y_b = conformer_conv_branch(p_conv, conv_w, conv_b, cnorm_w, cnorm_b, cpw_w, cpw_b)
    g_a, g_b = jnp.split(jax.nn.sigmoid(p_gate + gate_b.reshape(-1)), N_BRANCH, axis=-1)
    return (g_a * y_a + g_b * y_b) @ w_o


def moe_ffn(u, router_w, router_b, exp_gate, exp_up, exp_down, sh_gate, sh_up, sh_down):
    B, S, D = u.shape
    T = B * S
    f32 = jnp.float32
    ut = u.reshape(T, D)
    scores = jax.nn.sigmoid(ut.astype(f32) @ router_w.astype(f32))
    sel = scores + router_b.astype(f32)
    grp = sel.reshape(T, N_GROUPS, N_EXPERTS // N_GROUPS)
    grp_score = jnp.sum(lax.top_k(grp, 2)[0], -1)
    _, gidx = lax.top_k(grp_score, TOPK_GROUPS)
    rows = jnp.arange(T)[:, None]
    gmask = jnp.zeros((T, N_GROUPS), bool).at[rows, gidx].set(True)
    emask = jnp.repeat(gmask, N_EXPERTS // N_GROUPS, axis=1)
    _, eidx = lax.top_k(jnp.where(emask, sel, -jnp.inf), TOP_K)
    w_sel = jnp.take_along_axis(scores, eidx, axis=1)
    w_sel = w_sel / jnp.sum(w_sel, -1, keepdims=True) * ROUTED_SCALE
    gate = jnp.zeros((T, N_EXPERTS), f32).at[rows, eidx].set(w_sel)
    blk = math.gcd(T, MOE_BLOCK)

    def expert_block(args):
        ub, gb = args
        h = jax.nn.silu(jnp.einsum('td,edf->tef', ub, exp_gate)) * jnp.einsum('td,edf->tef', ub, exp_up)
        h = h * gb[..., None].astype(h.dtype)
        return jnp.einsum('tef,efd->td', h, exp_down)

    routed = lax.map(expert_block, (ut.reshape(T // blk, blk, D), gate.reshape(T // blk, blk, N_EXPERTS)))
    shared = (jax.nn.silu(ut @ sh_gate) * (ut @ sh_up)) @ sh_down
    return (routed.reshape(T, D).astype(u.dtype) + shared).reshape(B, S, D)


def setup_inputs(seed: int = 0) -> dict:
    key = jax.random.key(seed)
    ks = iter(jax.random.split(key, 48))
    L, D, E, F, Fs = DEPTH, D_MODEL, N_EXPERTS, D_EXPERT, D_SHARED
    beta = (8.0 * DEPTH) ** -0.25
    nrm = lambda shape, s: jax.random.normal(next(ks), shape, jnp.float32) * s
    uni = lambda shape, lo, hi: jax.random.uniform(next(ks), shape, jnp.float32, lo, hi)
    gain = lambda shape: 1.0 + nrm(shape, 0.05)
    return {
        "x": nrm((BATCH, SEQ, D), 1.0),
        "c": nrm((BATCH, D), 1.0),
        "ada_w": nrm((L, D, 6 * D), 0.5 * D ** -0.5),
        "ada_b": nrm((L, 6 * D), 0.02),
        "w_in": nrm((L, D, P_TOTAL), D ** -0.5),
        "shift_mu": uni((L, 2, RW_COLS), 0.0, 0.5),
        "w0": uni((L, 2, D_RWKV), -5.0, 1.0),
        "w_up": nrm((L, 2, LORA_W, D_RWKV), 0.1),
        "a0": nrm((L, 2, D_RWKV), 0.2),
        "a_up": nrm((L, 2, LORA_A, D_RWKV), 0.5 * LORA_A ** -0.5),
        "g_up": nrm((L, LORA_G, D_RWKV), LORA_G ** -0.5),
        "k_k": 0.85 + nrm((L, D_RWKV), 0.05),
        "k_a": gain((L, D_RWKV)),
        "r_k": nrm((L, RWKV_HEADS, RWKV_HEAD), 0.1),
        "lnx_w": gain((L, D_RWKV)),
        "lnx_b": nrm((L, D_RWKV), 0.02),
        "conv_w": nrm((L, CONV_WIDTH, D_CONV), CONV_WIDTH ** -0.5),
        "conv_b": nrm((L, D_CONV), 0.02),
        "cnorm_w": gain((L, D_CONV)),
        "cnorm_b": nrm((L, D_CONV), 0.02),
        "cpw_w": nrm((L, D_CONV, D), beta * D_CONV ** -0.5),
        "cpw_b": nrm((L, D), 0.02),
        "gate_b": nrm((L, N_BRANCH, D), 0.1),
        "w_o": nrm((L, D, D), beta * D ** -0.5),
        "ln1_w": gain((L, D)),
        "ln1_b": nrm((L, D), 0.02),
        "router_w": nrm((L, D, E), D ** -0.5),
        "router_b": nrm((L, E), 0.01),
        "exp_gate": nrm((L, E, D, F), D ** -0.5),
        "exp_up": nrm((L, E, D, F), D ** -0.5),
        "exp_down": nrm((L, E, F, D), beta * F ** -0.5),
        "sh_gate": nrm((L, D, Fs), D ** -0.5),
        "sh_up": nrm((L, D, Fs), D ** -0.5),
        "sh_down": nrm((L, Fs, D), beta * Fs ** -0.5),
        "ln2_w": gain((L, D)),
        "ln2_b": nrm((L, D), 0.02),
    }


def reference(x, c, ada_w, ada_b, w_in, shift_mu, w0, w_up, a0, a_up, g_up, k_k, k_a, r_k,
              lnx_w, lnx_b, conv_w, conv_b, cnorm_w, cnorm_b, cpw_w, cpw_b, gate_b, w_o,
              ln1_w, ln1_b, router_w, router_b, exp_gate, exp_up, exp_down,
              sh_gate, sh_up, sh_down, ln2_w, ln2_b):
    alpha = (2.0 * DEPTH) ** 0.25
    for l in range(DEPTH):
        mod = jax.nn.silu(c) @ ada_w[l] + ada_b[l]
        sh1, sc1, gt1, sh2, sc2, gt2 = jnp.split(mod[:, None, :], 6, axis=-1)
        u1 = layer_norm(x) * (1.0 + sc1) + sh1
        mix = token_mix(u1, w_in[l], shift_mu[l], w0[l], w_up[l], a0[l], a_up[l], g_up[l],
                        k_k[l], k_a[l], r_k[l], lnx_w[l], lnx_b[l], conv_w[l], conv_b[l],
                        cnorm_w[l], cnorm_b[l], cpw_w[l], cpw_b[l], gate_b[l], w_o[l])
        x = layer_norm(alpha * x + (1.0 + gt1) * mix, ln1_w[l], ln1_b[l])
        u2 = layer_norm(x) * (1.0 + sc2) + sh2
        ffn = moe_ffn(u2, router_w[l], router_b[l], exp_gate[l], exp_up[l], exp_down[l],
                      sh_gate[l], sh_up[l], sh_down[l])
        x = layer_norm(alpha * x + (1.0 + gt2) * ffn, ln2_w[l], ln2_b[l])
    return x
```

```python
import functools
import math

import jax
import jax.numpy as jnp
from jax import lax
from jax.experimental import pallas as pl
from jax.experimental.pallas import tpu as pltpu

F32 = jnp.float32
BF16 = jnp.bfloat16
HIGHEST = lax.Precision.HIGHEST

D_MODEL = 1024
HEAD = 64
HEADS = D_MODEL // HEAD
LORA_G = 128
RW_COLS = 3 * D_MODEL + 4 * 64 + LORA_G
DECAY_SCALE = math.exp(-0.5)
GN_EPS = 64e-5
LN_EPS = 1e-5
CONV_WIDTH = 31
CONV_PAD = CONV_WIDTH // 2
CONV_HALO = 16
N_EXPERTS = 64
TOP_K = 8
N_GROUPS = 8
GROUP_SIZE = N_EXPERTS // N_GROUPS
TOPK_GROUPS = 4
D_EXPERT = 256
ROUTED_SCALE = 2.5

SCAN_HG = 4
SCAN_W = SCAN_HG * HEAD
SCAN_CHUNK = 64
SCAN_R = SCAN_HG * SCAN_CHUNK
SCAN_BLOCK = 512

MOE_EXPERTS_PER_STEP = 4
VMEM_LIMIT = 56 * 1024 * 1024


def _mm(a, b):
    return jnp.dot(a.astype(BF16), b.astype(BF16), preferred_element_type=F32)


def _mm_nt(a, b):
    return lax.dot_general(a.astype(BF16), b.astype(BF16), (((1,), (1,)), ((), ())),
                           preferred_element_type=F32)


def _mm_tn(a, b):
    return lax.dot_general(a.astype(BF16), b.astype(BF16), (((0,), (0,)), ((), ())),
                           preferred_element_type=F32)


def _mm_f32(a, b):
    return jnp.dot(a, b, precision=HIGHEST, preferred_element_type=F32)


def _normalize(x):
    mu = jnp.mean(x, axis=-1, keepdims=True)
    xc = x - mu
    var = jnp.mean(xc * xc, axis=-1, keepdims=True)
    return xc * lax.rsqrt(var + LN_EPS)


def _sigmoid(x):
    return jax.nn.sigmoid(x)


def _silu(x):
    return x * jax.nn.sigmoid(x)


def _head_sum(x, seg, segt):
    return _mm_f32(_mm_f32(x, seg), segt)


def _adaln_kernel(c_ref, w_ref, b_ref, o_ref):
    o_ref[...] = _mm_f32(_silu(c_ref[...]), w_ref[...]) + b_ref[...]


def _adaln(c, ada_w, ada_b):
    b, d = c.shape
    n = ada_w.shape[1]
    rows = 8
    cp = jnp.zeros((rows, d), F32).at[:b].set(c)
    tn = 1536
    out = pl.pallas_call(
        _adaln_kernel,
        out_shape=jax.ShapeDtypeStruct((rows, n), F32),
        grid=(n // tn,),
        in_specs=[pl.BlockSpec((rows, d), lambda j: (0, 0)),
                  pl.BlockSpec((d, tn), lambda j: (0, j)),
                  pl.BlockSpec((1, tn), lambda j: (0, j))],
        out_specs=pl.BlockSpec((rows, tn), lambda j: (0, j)),
        name="adaln",
    )(cp, ada_w, ada_b.reshape(1, n))
    return out[:b].reshape(b, 6, d)


def _in_proj_kernel(x_ref, mod_ref, wrw_ref, wcv_ref, wgt_ref, prw_ref, pcv_ref, pgt_ref):
    m = mod_ref[0]
    u = (_normalize(x_ref[...]) * (1.0 + m[1:2]) + m[0:1]).astype(BF16)
    prw_ref[...] = jnp.dot(u, wrw_ref[...], preferred_element_type=F32)
    pcv_ref[...] = jnp.dot(u, wcv_ref[...], preferred_element_type=F32)
    pgt_ref[...] = jnp.dot(u, wgt_ref[...], preferred_element_type=F32)


def _in_proj(x2, mod, w_in, seq):
    t, d = x2.shape
    tm = 256
    w = w_in.astype(BF16)
    w_rw, w_cv, w_gt = w[:, :RW_COLS], w[:, RW_COLS:RW_COLS + 2 * d], w[:, RW_COLS + 2 * d:]
    resident = lambda shape: pl.BlockSpec(shape, lambda i: (0, 0), pipeline_mode=pl.Buffered(1))
    return pl.pallas_call(
        _in_proj_kernel,
        out_shape=(jax.ShapeDtypeStruct((t, RW_COLS), F32),
                   jax.ShapeDtypeStruct((t, 2 * d), F32),
                   jax.ShapeDtypeStruct((t, 2 * d), F32)),
        grid=(t // tm,),
        in_specs=[pl.BlockSpec((tm, d), lambda i: (i, 0)),
                  pl.BlockSpec((1, 6, d), lambda i: (i // (seq // tm), 0, 0)),
                  resident(w_rw.shape), resident(w_cv.shape), resident(w_gt.shape)],
        out_specs=(pl.BlockSpec((tm, RW_COLS), lambda i: (i, 0)),
                   pl.BlockSpec((tm, 2 * d), lambda i: (i, 0)),
                   pl.BlockSpec((tm, 2 * d), lambda i: (i, 0))),
        compiler_params=pltpu.CompilerParams(dimension_semantics=("parallel",),
                                             vmem_limit_bytes=VMEM_LIMIT),
        name="in_proj",
    )(x2, mod, w_rw, w_cv, w_gt)


def _rwkv_prep_kernel(p_ref, pp_ref, pn_ref, mu_ref, w0_ref, wup_ref, a0_ref, aup_ref, gup_ref,
                      kk_ref, ka_ref, rk_ref, seg_ref, segt_ref,
                      r_out, k_out, v_out, kk_out, lwf_out, lwb_out, af_out, ab_out, g_out, bonus_out,
                      *, blocks_per_seq, tb):
    d = D_MODEL
    i = pl.program_id(0)
    keep_prev = jnp.where(i % blocks_per_seq == 0, 0.0, 1.0)
    keep_next = jnp.where(i % blocks_per_seq == blocks_per_seq - 1, 0.0, 1.0)
    p = p_ref[...]
    prev_row = pp_ref[7:8, :] * keep_prev
    next_row = pn_ref[0:1, :] * keep_next
    rows = lax.broadcasted_iota(jnp.int32, (tb, 1), 0)
    p_prev = jnp.where(rows == 0, prev_row, pltpu.roll(p, 1, 0))
    p_next = jnp.where(rows == tb - 1, next_row, pltpu.roll(p, tb - 1, 0))
    ps = p + mu_ref[0:1, :] * (p_prev - p) + mu_ref[1:2, :] * (p_next - p)

    r, k, v = ps[:, :d], ps[:, d:2 * d], ps[:, 2 * d:3 * d]
    wl = ps[:, 3 * d:3 * d + 128]
    al = ps[:, 3 * d + 128:3 * d + 256]
    gl = ps[:, 3 * d + 256:3 * d + 384]
    seg, segt = seg_ref[...], segt_ref[...]

    lw = -DECAY_SCALE * _sigmoid(w0_ref[...] + _mm(jnp.tanh(wl), wup_ref[...]))
    a = _sigmoid(a0_ref[...] + _mm(al, aup_ref[...]))
    g = _mm(_sigmoid(gl), gup_ref[...])
    kraw = k * kk_ref[...]
    knorm = jnp.sqrt(_head_sum(kraw * kraw, seg, segt))
    kk = kraw / jnp.maximum(knorm, 1e-12)
    a_f = a[:, :d]
    k_f = k * (1.0 + (a_f - 1.0) * ka_ref[...])
    bonus = _head_sum(r * k_f * rk_ref[...], seg, segt) * v

    r_out[...] = r
    k_out[...] = k
    v_out[...] = v
    kk_out[...] = kk
    lwf_out[...] = lw[:, :d]
    lwb_out[...] = lw[:, d:]
    af_out[...] = a_f
    ab_out[...] = a[:, d:]
    g_out[...] = g
    bonus_out[...] = bonus


def _block_diag2(a, b):
    za = jnp.zeros((a.shape[0], b.shape[1]), a.dtype)
    zb = jnp.zeros((b.shape[0], a.shape[1]), a.dtype)
    return jnp.concatenate([jnp.concatenate([a, za], 1), jnp.concatenate([zb, b], 1)], 0)


def _head_indicator():
    ch = jnp.arange(D_MODEL) // HEAD
    seg = (ch[:, None] == jnp.arange(128)[None, :]).astype(F32)
    return seg, seg.T


def _rwkv_prep(p_rw, seq, shift_mu, w0, w_up, a0, a_up, g_up, k_k, k_a, r_k):
    t = p_rw.shape[0]
    d = D_MODEL
    tb = 256
    hb = 8
    seg, segt = _head_indicator()
    wup = _block_diag2(w_up[0], w_up[1]).astype(BF16)
    aup = _block_diag2(a_up[0], a_up[1]).astype(BF16)
    row = lambda x: x.reshape(1, -1)
    const = lambda shape: pl.BlockSpec(shape, lambda i: (0,) * len(shape))
    tok = pl.BlockSpec((tb, d), lambda i: (i, 0))
    outs = tuple(jax.ShapeDtypeStruct((t, d), F32) for _ in range(10))
    return pl.pallas_call(
        functools.partial(_rwkv_prep_kernel, blocks_per_seq=seq // tb, tb=tb),
        out_shape=outs,
        grid=(t // tb,),
        in_specs=[pl.BlockSpec((tb, RW_COLS), lambda i: (i, 0)),
                  pl.BlockSpec((hb, RW_COLS), lambda i: (jnp.maximum(i * (tb // hb) - 1, 0), 0)),
                  pl.BlockSpec((hb, RW_COLS), lambda i: (jnp.minimum((i + 1) * (tb // hb), t // hb - 1), 0)),
                  const((2, RW_COLS)), const((1, 2 * d)), const((128, 2 * d)), const((1, 2 * d)),
                  const((128, 2 * d)), const((LORA_G, d)), const((1, d)), const((1, d)), const((1, d)),
                  const((d, 128)), const((128, d))],
        out_specs=tuple(tok for _ in range(10)),
        compiler_params=pltpu.CompilerParams(dimension_semantics=("parallel",),
                                             vmem_limit_bytes=VMEM_LIMIT),
        name="rwkv_prep",
    )(p_rw, p_rw, p_rw, shift_mu, row(w0), wup, row(a0), aup, g_up.astype(BF16), row(k_k), row(k_a),
      row(r_k), seg, segt)


def _scan_chunk(forward, r, k, v, kk, lw, a, ka, m_ref, cm):
    c_steps, r_rows, w = SCAN_CHUNK, SCAN_R, SCAN_W
    tri = cm["tri_f"] if forward else cm["tri_b"]
    strict = cm["strict_f"] if forward else cm["strict_b"]
    read_mask = cm["incl_f"] if forward else cm["strict_b"]
    same = cm["same"]
    c = _mm_f32(tri, lw)
    ctot = c[c_steps - 1:c_steps, :] if forward else c[0:1, :]
    p_incl = jnp.exp(c)
    p_excl = jnp.exp(c - lw)
    p_inv = jnp.exp(-c)
    p_tail = jnp.exp(ctot - c)
    b = a * kk
    kd = k * (1.0 + (a - 1.0) * ka)
    kq = kk * p_excl
    rq = r * (p_incl if forward else p_excl)

    tile = lambda x: jnp.concatenate([x] * SCAN_HG, axis=0)
    expand = lambda x: jnp.where(same, tile(x), 0.0)
    fold = lambda x: (x[0:c_steps] + x[c_steps:2 * c_steps]) + (x[2 * c_steps:3 * c_steps] + x[3 * c_steps:])

    kq_e, rq_e, v_e = expand(kq), expand(rq), expand(v)
    sc = _mm_nt(jnp.concatenate([kq_e, rq_e], 0),
                jnp.concatenate([tile(b * p_inv), tile(kd * p_inv)], 0))
    low = jnp.where(strict, sc[:r_rows, :r_rows], 0.0)
    akk = jnp.where(strict, sc[:r_rows, r_rows:], 0.0)
    arb = jnp.where(read_mask, sc[r_rows:, :r_rows], 0.0)
    ark = jnp.where(read_mask, sc[r_rows:, r_rows:], 0.0)

    eye = cm["eye"]
    ld = jnp.where(cm["m8"], low, 0.0)
    t_inv = eye - ld
    l2 = _mm(ld, ld)
    t_inv = t_inv + _mm(t_inv, l2)
    t_inv = t_inv + _mm(t_inv, _mm(l2, l2))
    for off in cm["offs"]:
        t_inv = t_inv - _mm(_mm(t_inv, jnp.where(off, low, 0.0)), t_inv)

    z = _mm(t_inv, jnp.concatenate([kq_e, _mm(akk, v_e)], 1))
    kqp_ut = jnp.concatenate([z[:, :w], -z[:, w:]], 1)
    q2 = _mm(arb, kqp_ut)
    rqp = fold(rq_e - q2[:, :w])
    y0 = fold(q2[:, w:] + _mm(ark, v_e))
    g1 = _mm_tn(expand(b * p_tail), kqp_ut)
    a_mat = jnp.where(eye > 0.5, jnp.exp(ctot), 0.0) - g1[:, :w]
    b_mat = g1[:, w:] + _mm_tn(expand(kd * p_tail), v_e)
    m = m_ref[...]
    y = _mm(rqp, m) + y0
    m_ref[...] = _mm(a_mat, m) + b_mat
    return y


def _scan_kernel(rf, kf, vf, kkf, lwf, af, rb, kb, vb, kkb, lwb, ab, ka_ref, yf_ref, yb_ref,
                 mf_ref, mb_ref, *, n_chunks):
    @pl.when(pl.program_id(2) == 0)
    def _():
        mf_ref[...] = jnp.zeros_like(mf_ref)
        mb_ref[...] = jnp.zeros_like(mb_ref)

    c_steps, r_rows = SCAN_CHUNK, SCAN_R
    row = lax.broadcasted_iota(jnp.int32, (r_rows, r_rows), 0)
    col = lax.broadcasted_iota(jnp.int32, (r_rows, r_rows), 1)
    same = (row // c_steps) == (col // c_steps)
    ti = lax.broadcasted_iota(jnp.int32, (c_steps, c_steps), 0)
    tj = lax.broadcasted_iota(jnp.int32, (c_steps, c_steps), 1)
    cm = {
        "same": same,
        "strict_f": same & (col < row), "incl_f": same & (col <= row), "strict_b": same & (col > row),
        "m8": (row // 8) == (col // 8),
        "offs": [((row // (2 * s)) == (col // (2 * s))) & ((row // s) != (col // s)) for s in (8, 16, 32)],
        "eye": (row == col).astype(F32),
        "tri_f": (tj <= ti).astype(F32), "tri_b": (tj >= ti).astype(F32),
    }
    ka = ka_ref[...]

    def body(j, carry):
        of = pl.multiple_of(j * c_steps, c_steps)
        ob = pl.multiple_of((n_chunks - 1 - j) * c_steps, c_steps)
        sf, sb = pl.ds(of, c_steps), pl.ds(ob, c_steps)
        yf_ref[sf, :] = _scan_chunk(True, rf[sf, :], kf[sf, :], vf[sf, :], kkf[sf, :], lwf[sf, :],
                                    af[sf, :], ka, mf_ref, cm)
        yb_ref[sb, :] = _scan_chunk(False, rb[sb, :], kb[sb, :], vb[sb, :], kkb[sb, :], lwb[sb, :],
                                    ab[sb, :], ka, mb_ref, cm)
        return carry

    lax.fori_loop(0, n_chunks, body, 0)


def _wkv_scan(r, k, v, kk, lw_f, lw_b, a_f, a_b, k_a, batch, seq):
    t, d = r.shape
    ns = seq // SCAN_BLOCK
    fwd = pl.BlockSpec((SCAN_BLOCK, SCAN_W), lambda b, g, s: (b * ns + s, g))
    bwd = pl.BlockSpec((SCAN_BLOCK, SCAN_W), lambda b, g, s: (b * ns + ns - 1 - s, g))
    return pl.pallas_call(
        functools.partial(_scan_kernel, n_chunks=SCAN_BLOCK // SCAN_CHUNK),
        out_shape=(jax.ShapeDtypeStruct((t, d), F32), jax.ShapeDtypeStruct((t, d), F32)),
        grid=(batch, d // SCAN_W, ns),
        in_specs=[fwd] * 6 + [bwd] * 6 + [pl.BlockSpec((1, SCAN_W), lambda b, g, s: (0, g))],
        out_specs=(fwd, bwd),
        scratch_shapes=[pltpu.VMEM((SCAN_W, SCAN_W), F32), pltpu.VMEM((SCAN_W, SCAN_W), F32)],
        compiler_params=pltpu.CompilerParams(dimension_semantics=("parallel", "parallel", "arbitrary"),
                                             vmem_limit_bytes=VMEM_LIMIT),
        name="wkv_scan",
    )(r, k, v, kk, lw_f, a_f, r, k, v, kk, lw_b, a_b, k_a.reshape(1, d))


def _mix_post_kernel(yf_ref, yb_ref, g_ref, bonus_ref, pc_ref, pcp_ref, pcn_ref, pg_ref, x_ref, mod_ref,
                     lnxw_ref, lnxb_ref, convw_ref, convb_ref, cnw_ref, cnb_ref, cpw_ref, cpb_ref,
                     gateb_ref, wo_ref, ln1w_ref, ln1b_ref, seg_ref, segt_ref,
                     x1_ref, u2_ref, hext_ref, *, blocks_per_seq, tb, alpha):
    d = D_MODEL
    i = pl.program_id(0)
    keep_prev = jnp.where(i % blocks_per_seq == 0, 0.0, 1.0)
    keep_next = jnp.where(i % blocks_per_seq == blocks_per_seq - 1, 0.0, 1.0)
    glu = lambda p: p[:, :d] * _sigmoid(p[:, d:])

    hext_ref[0:CONV_HALO, :] = glu(pcp_ref[...]) * keep_prev
    hext_ref[CONV_HALO:CONV_HALO + tb, :] = glu(pc_ref[...])
    hext_ref[CONV_HALO + tb:2 * CONV_HALO + tb, :] = glu(pcn_ref[...]) * keep_next
    acc = jnp.zeros((tb, d), F32) + convb_ref[...]
    for j in range(CONV_WIDTH):
        acc = acc + convw_ref[j:j + 1, :] * hext_ref[pl.ds(CONV_HALO - CONV_PAD + j, tb), :]
    hc = _silu(_normalize(acc) * cnw_ref[...] + cnb_ref[...])
    y_conv = _mm(hc, cpw_ref[...]) + cpb_ref[...]

    seg, segt = seg_ref[...], segt_ref[...]
    y = yf_ref[...] + yb_ref[...]
    mu = _head_sum(y, seg, segt) * (1.0 / HEAD)
    yc = y - mu
    var = _head_sum(yc * yc, seg, segt) * (1.0 / HEAD)
    y_rwkv = (yc * lax.rsqrt(var + GN_EPS) * lnxw_ref[...] + lnxb_ref[...] + bonus_ref[...]) * g_ref[...]

    gates = _sigmoid(pg_ref[...] + gateb_ref[...])
    mix = _mm(gates[:, :d] * y_rwkv + gates[:, d:] * y_conv, wo_ref[...])
    m = mod_ref[0]
    x1 = _normalize(alpha * x_ref[...] + (1.0 + m[2:3]) * mix) * ln1w_ref[...] + ln1b_ref[...]
    x1_ref[...] = x1
    u2_ref[...] = _normalize(x1) * (1.0 + m[4:5]) + m[3:4]


def _mix_post(y_f, y_b, g, bonus, p_conv, p_gate, x2, mod, seq, alpha, lnx_w, lnx_b, conv_w, conv_b,
              cnorm_w, cnorm_b, cpw_w, cpw_b, gate_b, w_o, ln1_w, ln1_b):
    t, d = x2.shape
    tb = 256
    hb = CONV_HALO
    seg, segt = _head_indicator()
    row = lambda x: x.reshape(1, -1)
    const = lambda shape: pl.BlockSpec(shape, lambda i: (0,) * len(shape))
    tok = pl.BlockSpec((tb, d), lambda i: (i, 0))
    tok2 = pl.BlockSpec((tb, 2 * d), lambda i: (i, 0))
    return pl.pallas_call(
        functools.partial(_mix_post_kernel, blocks_per_seq=seq // tb, tb=tb, alpha=alpha),
        out_shape=(jax.ShapeDtypeStruct((t, d), F32), jax.ShapeDtypeStruct((t, d), F32)),
        grid=(t // tb,),
        in_specs=[tok, tok, tok, tok, tok2,
                  pl.BlockSpec((hb, 2 * d), lambda i: (jnp.maximum(i * (tb // hb) - 1, 0), 0)),
                  pl.BlockSpec((hb, 2 * d), lambda i: (jnp.minimum((i + 1) * (tb // hb), t // hb - 1), 0)),
                  tok2, tok,
                  pl.BlockSpec((1, 6, d), lambda i: (i // (seq // tb), 0, 0)),
                  const((1, d)), const((1, d)), const((CONV_WIDTH, d)), const((1, d)), const((1, d)),
                  const((1, d)), const((d, d)), const((1, d)), const((1, 2 * d)), const((d, d)),
                  const((1, d)), const((1, d)), const((d, 128)), const((128, d))],
        out_specs=(tok, tok),
        scratch_shapes=[pltpu.VMEM((tb + 2 * CONV_HALO, d), F32)],
        compiler_params=pltpu.CompilerParams(dimension_semantics=("parallel",),
                                             vmem_limit_bytes=VMEM_LIMIT),
        name="mix_post",
    )(y_f, y_b, g, bonus, p_conv, p_conv, p_conv, p_gate, x2, mod,
      row(lnx_w), row(lnx_b), conv_w, row(conv_b), row(cnorm_w), row(cnorm_b), cpw_w.astype(BF16),
      row(cpw_b), row(gate_b), w_o.astype(BF16), row(ln1_w), row(ln1_b), seg, segt)


def _router_kernel(u_ref, rwt_ref, rb_ref, gate_ref, *, tb):
    neg = -jnp.inf
    logits = lax.dot_general(rwt_ref[...], u_ref[...], (((1,), (1,)), ((), ())),
                             precision=HIGHEST, preferred_element_type=F32)
    scores = _sigmoid(logits)
    sel = scores + rb_ref[...]
    sub = lax.broadcasted_iota(jnp.int32, (GROUP_SIZE, tb), 0)
    groups = [sel[GROUP_SIZE * g:GROUP_SIZE * (g + 1), :] for g in range(N_GROUPS)]

    gscore = []
    for xg in groups:
        m1 = jnp.max(xg, axis=0, keepdims=True)
        first = jnp.min(jnp.where(xg == m1, sub, GROUP_SIZE), axis=0, keepdims=True)
        m2 = jnp.max(jnp.where(sub == first, neg, xg), axis=0, keepdims=True)
        gscore.append(m1 + m2)
    cur = []
    for g, xg in enumerate(groups):
        rank = jnp.zeros((1, tb), jnp.int32)
        for g2 in range(N_GROUPS):
            if g2 == g:
                continue
            ahead = gscore[g2] > gscore[g]
            if g2 < g:
                ahead = ahead | (gscore[g2] == gscore[g])
            rank = rank + ahead.astype(jnp.int32)
        cur.append(jnp.where(rank < TOPK_GROUPS, xg, neg))
    chosen = [jnp.zeros((GROUP_SIZE, tb), jnp.bool_) for _ in range(N_GROUPS)]
    for _ in range(TOP_K):
        best = cur[0]
        for xg in cur[1:]:
            best = jnp.maximum(best, xg)
        best = jnp.max(best, axis=0, keepdims=True)
        cand = jnp.full((GROUP_SIZE, tb), N_EXPERTS, jnp.int32)
        for g, xg in enumerate(cur):
            cand = jnp.minimum(cand, jnp.where(xg == best, sub + GROUP_SIZE * g, N_EXPERTS))
        pick = jnp.min(cand, axis=0, keepdims=True)
        for g in range(N_GROUPS):
            hit = (sub + GROUP_SIZE * g) == pick
            chosen[g] = chosen[g] | hit
            cur[g] = jnp.where(hit, neg, cur[g])
    picked = jnp.concatenate(
        [jnp.where(chosen[g], scores[GROUP_SIZE * g:GROUP_SIZE * (g + 1), :], 0.0) for g in range(N_GROUPS)], 0)
    gate_t = picked / jnp.sum(picked, axis=0, keepdims=True) * ROUTED_SCALE

    ident = (lax.broadcasted_iota(jnp.int32, (tb, tb), 0)
             == lax.broadcasted_iota(jnp.int32, (tb, tb), 1)).astype(BF16)
    rest = gate_t
    out = jnp.zeros((tb, N_EXPERTS), F32)
    for _ in range(3):
        part = rest.astype(BF16)
        rest = rest - part.astype(F32)
        out = out + lax.dot_general(ident, part, (((1,), (1,)), ((), ())), preferred_element_type=F32)
    gate_ref[...] = out


def _router(u2, router_w, router_b):
    t, d = u2.shape
    tb = 512
    return pl.pallas_call(
        functools.partial(_router_kernel, tb=tb),
        out_shape=jax.ShapeDtypeStruct((t, N_EXPERTS), F32),
        grid=(t // tb,),
        in_specs=[pl.BlockSpec((tb, d), lambda i: (i, 0)),
                  pl.BlockSpec((N_EXPERTS, d), lambda i: (0, 0)),
                  pl.BlockSpec((N_EXPERTS, 1), lambda i: (0, 0))],
        out_specs=pl.BlockSpec((tb, N_EXPERTS), lambda i: (i, 0)),
        compiler_params=pltpu.CompilerParams(dimension_semantics=("parallel",),
                                             vmem_limit_bytes=VMEM_LIMIT),
        name="router",
    )(u2, router_w.T, router_b.reshape(N_EXPERTS, 1))


def _moe_kernel(u_ref, gate_ref, x1_ref, mod_ref, wg_ref, wu_ref, wd_ref, shg_ref, shu_ref, shd_ref,
                ln2w_ref, ln2b_ref, out_ref, ub_ref, acc_ref, *, alpha, n_steps):
    e = pl.program_id(1)

    @pl.when(e == 0)
    def _():
        ub = u_ref[...].astype(BF16)
        ub_ref[...] = ub
        hs = _silu(jnp.dot(ub, shg_ref[...], preferred_element_type=F32)) * jnp.dot(
            ub, shu_ref[...], preferred_element_type=F32)
        acc_ref[...] = _mm(hs, shd_ref[...])

    ub = ub_ref[...]
    gate = gate_ref[0]
    acc = acc_ref[...]
    for j in range(MOE_EXPERTS_PER_STEP):
        h = _silu(jnp.dot(ub, wg_ref[j], preferred_element_type=F32)) * jnp.dot(
            ub, wu_ref[j], preferred_element_type=F32)
        acc = acc + _mm(h * gate[:, j:j + 1], wd_ref[j])
    acc_ref[...] = acc

    @pl.when(e == n_steps - 1)
    def _():
        m = mod_ref[0]
        out_ref[...] = (_normalize(alpha * x1_ref[...] + (1.0 + m[5:6]) * acc_ref[...]) * ln2w_ref[...]
                        + ln2b_ref[...])


def _moe(u2, gate, x1, mod, seq, alpha, exp_gate, exp_up, exp_down, sh_gate, sh_up, sh_down, ln2_w, ln2_b):
    t, d = u2.shape
    tb = 512
    eps = MOE_EXPERTS_PER_STEP
    n_steps = N_EXPERTS // eps
    gate_g = gate.reshape(t, n_steps, eps).transpose(1, 0, 2)
    row = lambda x: x.reshape(1, -1)
    const = lambda shape: pl.BlockSpec(shape, lambda i, e: (0,) * len(shape))
    tok = pl.BlockSpec((tb, d), lambda i, e: (i, 0))
    return pl.pallas_call(
        functools.partial(_moe_kernel, alpha=alpha, n_steps=n_steps),
        out_shape=jax.ShapeDtypeStruct((t, d), F32),
        grid=(t // tb, n_steps),
        in_specs=[tok,
                  pl.BlockSpec((1, tb, eps), lambda i, e: (e, i, 0)),
                  tok,
                  pl.BlockSpec((1, 6, d), lambda i, e: (i // (seq // tb), 0, 0)),
                  pl.BlockSpec((eps, d, D_EXPERT), lambda i, e: (e, 0, 0)),
                  pl.BlockSpec((eps, d, D_EXPERT), lambda i, e: (e, 0, 0)),
                  pl.BlockSpec((eps, D_EXPERT, d), lambda i, e: (e, 0, 0)),
                  const(sh_gate.shape), const(sh_up.shape), const(sh_down.shape),
                  const((1, d)), const((1, d))],
        out_specs=tok,
        scratch_shapes=[pltpu.VMEM((tb, d), BF16), pltpu.VMEM((tb, d), F32)],
        compiler_params=pltpu.CompilerParams(dimension_semantics=("parallel", "arbitrary"),
                                             vmem_limit_bytes=VMEM_LIMIT),
        name="moe",
    )(u2, gate_g, x1, mod, exp_gate.astype(BF16), exp_up.astype(BF16), exp_down.astype(BF16),
      sh_gate.astype(BF16), sh_up.astype(BF16), sh_down.astype(BF16), row(ln2_w), row(ln2_b))


def kernel(x, c, ada_w, ada_b, w_in, shift_mu, w0, w_up, a0, a_up, g_up, k_k, k_a, r_k, lnx_w, lnx_b,
           conv_w, conv_b, cnorm_w, cnorm_b, cpw_w, cpw_b, gate_b, w_o, ln1_w, ln1_b, router_w, router_b,
           exp_gate, exp_up, exp_down, sh_gate, sh_up, sh_down, ln2_w, ln2_b):
    batch, seq, d = x.shape
    depth = ada_w.shape[0]
    alpha = (2.0 * depth) ** 0.25
    x2 = x.reshape(batch * seq, d)
    for l in range(depth):
        mod = _adaln(c, ada_w[l], ada_b[l])
        p_rw, p_conv, p_gate = _in_proj(x2, mod, w_in[l], seq)
        r, k, v, kk, lw_f, lw_b, a_f, a_b, g, bonus = _rwkv_prep(
            p_rw, seq, shift_mu[l], w0[l], w_up[l], a0[l], a_up[l], g_up[l], k_k[l], k_a[l], r_k[l])
        y_f, y_b = _wkv_scan(r, k, v, kk, lw_f, lw_b, a_f, a_b, k_a[l], batch, seq)
        x1, u2 = _mix_post(y_f, y_b, g, bonus, p_conv, p_gate, x2, mod, seq, alpha, lnx_w[l], lnx_b[l],
                           conv_w[l], conv_b[l], cnorm_w[l], cnorm_b[l], cpw_w[l], cpw_b[l], gate_b[l],
                           w_o[l], ln1_w[l], ln1_b[l])
        gate = _router(u2, router_w[l], router_b[l])
        x2 = _moe(u2, gate, x1, mod, seq, alpha, exp_gate[l], exp_up[l], exp_down[l], sh_gate[l],
                  sh_up[l], sh_down[l], ln2_w[l], ln2_b[l])
    return x2.reshape(batch, seq, d)
```

```python
import functools
import math

import jax
import jax.numpy as jnp
from jax import lax
from jax.experimental import pallas as pl
from jax.experimental.pallas import tpu as pltpu

F32 = jnp.float32
BF16 = jnp.bfloat16
HIGHEST = lax.Precision.HIGHEST

D_MODEL = 1024
HEAD = 64
HEADS = D_MODEL // HEAD
LORA_G = 128
RW_COLS = 3 * D_MODEL + 4 * 64 + LORA_G
DECAY_SCALE = math.exp(-0.5)
GN_EPS = 64e-5
LN_EPS = 1e-5
CONV_WIDTH = 31
CONV_PAD = CONV_WIDTH // 2
CONV_HALO = 16
N_EXPERTS = 64
TOP_K = 8
N_GROUPS = 8
GROUP_SIZE = N_EXPERTS // N_GROUPS
TOPK_GROUPS = 4
D_EXPERT = 256
ROUTED_SCALE = 2.5

SCAN_HG = 2
SCAN_W = SCAN_HG * HEAD
SCAN_CHUNK = 64
SCAN_R = SCAN_HG * SCAN_CHUNK
SCAN_GROUPS = 8
SCAN_BLOCK = 256

MOE_EXPERTS_PER_STEP = 4
VMEM_LIMIT = 56 * 1024 * 1024


def _mm(a, b):
    return jnp.dot(a.astype(BF16), b.astype(BF16), preferred_element_type=F32)


def _mm_nt(a, b):
    return lax.dot_general(a.astype(BF16), b.astype(BF16), (((1,), (1,)), ((), ())),
                           preferred_element_type=F32)


def _mm_tn(a, b):
    return lax.dot_general(a.astype(BF16), b.astype(BF16), (((0,), (0,)), ((), ())),
                           preferred_element_type=F32)


def _mm_f32(a, b):
    return jnp.dot(a, b, precision=HIGHEST, preferred_element_type=F32)


def _normalize(x):
    mu = jnp.mean(x, axis=-1, keepdims=True)
    xc = x - mu
    var = jnp.mean(xc * xc, axis=-1, keepdims=True)
    return xc * lax.rsqrt(var + LN_EPS)


def _sigmoid(x):
    return jax.nn.sigmoid(x)


def _silu(x):
    return x * jax.nn.sigmoid(x)


def _split_bf16(x):
    hi = x.astype(BF16)
    return hi, (x - hi.astype(F32)).astype(BF16)


def _mm_split(x, w01):
    hi, lo = _split_bf16(x)
    return (jnp.dot(hi, w01, preferred_element_type=F32) + jnp.dot(lo, w01, preferred_element_type=F32))


def _head_sum(x, seg, segt):
    return _mm_split(_mm_split(x, seg), segt)


def _adaln_kernel(c_ref, w_ref, b_ref, o_ref):
    o_ref[...] = _mm_f32(_silu(c_ref[...]), w_ref[...]) + b_ref[...]


def _adaln(c, ada_w, ada_b):
    b, d = c.shape
    n = ada_w.shape[1]
    rows = 8
    cp = jnp.zeros((rows, d), F32).at[:b].set(c)
    tn = 1536
    out = pl.pallas_call(
        _adaln_kernel,
        out_shape=jax.ShapeDtypeStruct((rows, n), F32),
        grid=(n // tn,),
        in_specs=[pl.BlockSpec((rows, d), lambda j: (0, 0)),
                  pl.BlockSpec((d, tn), lambda j: (0, j)),
                  pl.BlockSpec((1, tn), lambda j: (0, j))],
        out_specs=pl.BlockSpec((rows, tn), lambda j: (0, j)),
        name="adaln",
    )(cp, ada_w, ada_b.reshape(1, n))
    return out[:b].reshape(b, 6, d)


def _in_proj_kernel(x_ref, mod_ref, wrw_ref, wcv_ref, wgt_ref, prw_ref, pcv_ref, pgt_ref):
    m = mod_ref[0]
    u = (_normalize(x_ref[...]) * (1.0 + m[1:2]) + m[0:1]).astype(BF16)
    prw_ref[...] = jnp.dot(u, wrw_ref[...], preferred_element_type=F32)
    pcv_ref[...] = jnp.dot(u, wcv_ref[...], preferred_element_type=F32)
    pgt_ref[...] = jnp.dot(u, wgt_ref[...], preferred_element_type=F32)


def _in_proj(x2, mod, w_in, seq):
    t, d = x2.shape
    tm = 256
    w = w_in.astype(BF16)
    w_rw, w_cv, w_gt = w[:, :RW_COLS], w[:, RW_COLS:RW_COLS + 2 * d], w[:, RW_COLS + 2 * d:]
    resident = lambda shape: pl.BlockSpec(shape, lambda i: (0, 0), pipeline_mode=pl.Buffered(1))
    return pl.pallas_call(
        _in_proj_kernel,
        out_shape=(jax.ShapeDtypeStruct((t, RW_COLS), F32),
                   jax.ShapeDtypeStruct((t, 2 * d), F32),
                   jax.ShapeDtypeStruct((t, 2 * d), F32)),
        grid=(t // tm,),
        in_specs=[pl.BlockSpec((tm, d), lambda i: (i, 0)),
                  pl.BlockSpec((1, 6, d), lambda i: (i // (seq // tm), 0, 0)),
                  resident(w_rw.shape), resident(w_cv.shape), resident(w_gt.shape)],
        out_specs=(pl.BlockSpec((tm, RW_COLS), lambda i: (i, 0)),
                   pl.BlockSpec((tm, 2 * d), lambda i: (i, 0)),
                   pl.BlockSpec((tm, 2 * d), lambda i: (i, 0))),
        compiler_params=pltpu.CompilerParams(dimension_semantics=("parallel",),
                                             vmem_limit_bytes=VMEM_LIMIT),
        name="in_proj",
    )(x2, mod, w_rw, w_cv, w_gt)


def _rwkv_prep_kernel(p_ref, pp_ref, pn_ref, mu_ref, w0_ref, wup_ref, a0_ref, aup_ref, gup_ref,
                      kk_ref, ka_ref, rk_ref, seg_ref, segt_ref,
                      r_out, k_out, v_out, kk_out, lwf_out, lwb_out, af_out, ab_out, g_out, bonus_out,
                      *, blocks_per_seq, tb):
    d = D_MODEL
    i = pl.program_id(0)
    keep_prev = jnp.where(i % blocks_per_seq == 0, 0.0, 1.0)
    keep_next = jnp.where(i % blocks_per_seq == blocks_per_seq - 1, 0.0, 1.0)
    p = p_ref[...]
    prev_row = pp_ref[7:8, :] * keep_prev
    next_row = pn_ref[0:1, :] * keep_next
    rows = lax.broadcasted_iota(jnp.int32, (tb, 1), 0)
    p_prev = jnp.where(rows == 0, prev_row, pltpu.roll(p, 1, 0))
    p_next = jnp.where(rows == tb - 1, next_row, pltpu.roll(p, tb - 1, 0))
    ps = p + mu_ref[0:1, :] * (p_prev - p) + mu_ref[1:2, :] * (p_next - p)

    r, k, v = ps[:, :d], ps[:, d:2 * d], ps[:, 2 * d:3 * d]
    wl = ps[:, 3 * d:3 * d + 128]
    al = ps[:, 3 * d + 128:3 * d + 256]
    gl = ps[:, 3 * d + 256:3 * d + 384]
    seg, segt = seg_ref[...], segt_ref[...]

    lw = -DECAY_SCALE * _sigmoid(w0_ref[...] + _mm(jnp.tanh(wl), wup_ref[...]))
    a = _sigmoid(a0_ref[...] + _mm(al, aup_ref[...]))
    g = _mm(_sigmoid(gl), gup_ref[...])
    kraw = k * kk_ref[...]
    knorm = jnp.sqrt(_head_sum(kraw * kraw, seg, segt))
    kk = kraw / jnp.maximum(knorm, 1e-12)
    a_f = a[:, :d]
    k_f = k * (1.0 + (a_f - 1.0) * ka_ref[...])
    bonus = _head_sum(r * k_f * rk_ref[...], seg, segt) * v

    r_out[...] = r
    k_out[...] = k
    v_out[...] = v
    kk_out[...] = kk
    lwf_out[...] = lw[:, :d]
    lwb_out[...] = lw[:, d:]
    af_out[...] = a_f
    ab_out[...] = a[:, d:]
    g_out[...] = g
    bonus_out[...] = bonus


def _block_diag2(a, b):
    za = jnp.zeros((a.shape[0], b.shape[1]), a.dtype)
    zb = jnp.zeros((b.shape[0], a.shape[1]), a.dtype)
    return jnp.concatenate([jnp.concatenate([a, za], 1), jnp.concatenate([zb, b], 1)], 0)


def _head_indicator():
    ch = jnp.arange(D_MODEL) // HEAD
    seg = (ch[:, None] == jnp.arange(128)[None, :]).astype(BF16)
    return seg, seg.T


def _rwkv_prep(p_rw, seq, shift_mu, w0, w_up, a0, a_up, g_up, k_k, k_a, r_k):
    t = p_rw.shape[0]
    d = D_MODEL
    tb = 256
    hb = 8
    seg, segt = _head_indicator()
    wup = _block_diag2(w_up[0], w_up[1]).astype(BF16)
    aup = _block_diag2(a_up[0], a_up[1]).astype(BF16)
    row = lambda x: x.reshape(1, -1)
    const = lambda shape: pl.BlockSpec(shape, lambda i: (0,) * len(shape))
    tok = pl.BlockSpec((tb, d), lambda i: (i, 0))
    outs = tuple(jax.ShapeDtypeStruct((t, d), F32) for _ in range(10))
    return pl.pallas_call(
        functools.partial(_rwkv_prep_kernel, blocks_per_seq=seq // tb, tb=tb),
        out_shape=outs,
        grid=(t // tb,),
        in_specs=[pl.BlockSpec((tb, RW_COLS), lambda i: (i, 0)),
                  pl.BlockSpec((hb, RW_COLS), lambda i: (jnp.maximum(i * (tb // hb) - 1, 0), 0)),
                  pl.BlockSpec((hb, RW_COLS), lambda i: (jnp.minimum((i + 1) * (tb // hb), t // hb - 1), 0)),
                  const((2, RW_COLS)), const((1, 2 * d)), const((128, 2 * d)), const((1, 2 * d)),
                  const((128, 2 * d)), const((LORA_G, d)), const((1, d)), const((1, d)), const((1, d)),
                  const((d, 128)), const((128, d))],
        out_specs=tuple(tok for _ in range(10)),
        compiler_params=pltpu.CompilerParams(dimension_semantics=("parallel",),
                                             vmem_limit_bytes=VMEM_LIMIT),
        name="rwkv_prep",
    )(p_rw, p_rw, p_rw, shift_mu, row(w0), wup, row(a0), aup, g_up.astype(BF16), row(k_k), row(k_a),
      row(r_k), seg, segt)


def _scan_chunk(forward, r, k, v, kk, lw, a, ka, m, cm):
    c_steps, r_rows, w = SCAN_CHUNK, SCAN_R, SCAN_W
    strict = cm["strict_f"] if forward else cm["strict_b"]
    read_mask = cm["incl_f"] if forward else cm["strict_b"]
    same = cm["same"]
    c = lw
    step = 1
    while step < c_steps:
        if forward:
            c = c + jnp.where(cm["t_idx"] >= step, pltpu.roll(c, step, 0), 0.0)
        else:
            c = c + jnp.where(cm["t_idx"] < c_steps - step, pltpu.roll(c, c_steps - step, 0), 0.0)
        step *= 2
    ctot = c[c_steps - 1:c_steps, :] if forward else c[0:1, :]
    p_incl = jnp.exp(c)
    p_excl = jnp.exp(c - lw)
    p_inv = jnp.exp(-c)
    p_tail = jnp.exp(ctot - c)
    b = a * kk
    kd = k * (1.0 + (a - 1.0) * ka)
    kq = kk * p_excl
    rq = r * (p_incl if forward else p_excl)

    tile = lambda x: jnp.concatenate([x] * SCAN_HG, axis=0)
    expand = lambda x: jnp.where(same, tile(x), 0.0)
    fold = lambda x: sum(x[h * c_steps:(h + 1) * c_steps] for h in range(1, SCAN_HG)) + x[0:c_steps]

    kq_e, rq_e, v_e = expand(kq), expand(rq), expand(v)
    sc = _mm_nt(jnp.concatenate([kq_e, rq_e], 0),
                jnp.concatenate([tile(b * p_inv), tile(kd * p_inv)], 0))
    yield
    low = jnp.where(strict, sc[:r_rows, :r_rows], 0.0)
    akk = jnp.where(strict, sc[:r_rows, r_rows:], 0.0)
    arb = jnp.where(read_mask, sc[r_rows:, :r_rows], 0.0)
    ark = jnp.where(read_mask, sc[r_rows:, r_rows:], 0.0)

    eye = cm["eye"]
    ld = jnp.where(cm["m8"], low, 0.0)
    t_inv = eye - ld
    l2 = _mm(ld, ld)
    on_v = _mm(jnp.concatenate([akk.astype(BF16), ark.astype(BF16), expand(kd * p_tail).astype(BF16).T], 0), v_e)
    akk_v, ark_v, kkp_v = on_v[:r_rows], on_v[r_rows:2 * r_rows], on_v[2 * r_rows:]
    yield
    on_l2 = _mm(jnp.concatenate([t_inv, l2], 0), l2)
    t_inv = t_inv + on_l2[:r_rows]
    l4 = on_l2[r_rows:]
    yield
    t_inv = t_inv + _mm(t_inv, l4)
    yield
    for off in cm["offs"]:
        t_lo = _mm(t_inv, jnp.where(off, low, 0.0))
        yield
        t_inv = t_inv - _mm(t_lo, t_inv)
        yield

    z = _mm(t_inv, jnp.concatenate([kq_e, akk_v], 1))
    yield
    kqp_ut = jnp.concatenate([z[:, :w], -z[:, w:]], 1)
    on_ku = _mm(jnp.concatenate([arb.astype(BF16), expand(b * p_tail).astype(BF16).T], 0), kqp_ut)
    q2, g1 = on_ku[:r_rows], on_ku[r_rows:]
    yield
    rqp = fold(rq_e - q2[:, :w])
    y0 = fold(q2[:, w:] + ark_v)
    a_mat = jnp.where(eye > 0.5, jnp.exp(ctot), 0.0) - g1[:, :w]
    b_mat = g1[:, w:] + kkp_v
    on_m = _mm(jnp.concatenate([rqp, a_mat], 0), m)
    return on_m[:c_steps] + y0, on_m[c_steps:] + b_mat


def _interleave(gens):
    results = [None] * len(gens)
    active = list(range(len(gens)))
    while active:
        for i in list(active):
            try:
                next(gens[i])
            except StopIteration as done:
                results[i] = done.value
                active.remove(i)
    return results


def _scan_kernel(rf, kf, vf, kkf, lwf, af, rb, kb, vb, kkb, lwb, ab, ka_ref, yf_ref, yb_ref,
                 mf_ref, mb_ref, *, n_chunks):
    @pl.when(pl.program_id(2) == 0)
    def _():
        mf_ref[...] = jnp.zeros_like(mf_ref)
        mb_ref[...] = jnp.zeros_like(mb_ref)

    c_steps, r_rows = SCAN_CHUNK, SCAN_R
    row = lax.broadcasted_iota(jnp.int32, (r_rows, r_rows), 0)
    col = lax.broadcasted_iota(jnp.int32, (r_rows, r_rows), 1)
    same = (row // c_steps) == (col // c_steps)
    cm = {
        "same": same,
        "strict_f": same & (col < row), "incl_f": same & (col <= row), "strict_b": same & (col > row),
        "m8": (row // 8) == (col // 8),
        "offs": [((row // (2 * s)) == (col // (2 * s))) & ((row // s) != (col // s)) for s in (8, 16, 32)],
        "eye": (row == col).astype(F32),
        "t_idx": lax.broadcasted_iota(jnp.int32, (c_steps, SCAN_W), 0),
    }
    ka = ka_ref[...]

    def body(j, carry):
        of = pl.multiple_of(j * c_steps, c_steps)
        ob = pl.multiple_of((n_chunks - 1 - j) * c_steps, c_steps)
        sf, sb = pl.ds(of, c_steps), pl.ds(ob, c_steps)
        lanes = [slice(g * SCAN_W, (g + 1) * SCAN_W) for g in range(SCAN_GROUPS)]
        ins_f = [[ref[sf, ln] for ref in (rf, kf, vf, kkf, lwf, af)] + [ka[:, ln], mf_ref[g]]
                 for g, ln in enumerate(lanes)]
        ins_b = [[ref[sb, ln] for ref in (rb, kb, vb, kkb, lwb, ab)] + [ka[:, ln], mb_ref[g]]
                 for g, ln in enumerate(lanes)]
        outs = _interleave([_scan_chunk(True, *args, cm) for args in ins_f]
                           + [_scan_chunk(False, *args, cm) for args in ins_b])
        for g, ln in enumerate(lanes):
            yf_ref[sf, ln], mf_ref[g] = outs[g]
            yb_ref[sb, ln], mb_ref[g] = outs[SCAN_GROUPS + g]
        return carry

    lax.fori_loop(0, n_chunks, body, 0)


def _wkv_scan(r, k, v, kk, lw_f, lw_b, a_f, a_b, k_a, batch, seq):
    t, d = r.shape
    ns = seq // SCAN_BLOCK
    lanes = SCAN_GROUPS * SCAN_W
    fwd = pl.BlockSpec((SCAN_BLOCK, lanes), lambda b, g, s: (b * ns + s, g))
    bwd = pl.BlockSpec((SCAN_BLOCK, lanes), lambda b, g, s: (b * ns + ns - 1 - s, g))
    state = pltpu.VMEM((SCAN_GROUPS, SCAN_W, SCAN_W), F32)
    return pl.pallas_call(
        functools.partial(_scan_kernel, n_chunks=SCAN_BLOCK // SCAN_CHUNK),
        out_shape=(jax.ShapeDtypeStruct((t, d), F32), jax.ShapeDtypeStruct((t, d), F32)),
        grid=(batch, d // lanes, ns),
        in_specs=[fwd] * 6 + [bwd] * 6 + [pl.BlockSpec((1, lanes), lambda b, g, s: (0, g))],
        out_specs=(fwd, bwd),
        scratch_shapes=[state, state],
        compiler_params=pltpu.CompilerParams(dimension_semantics=("parallel", "parallel", "arbitrary"),
                                             vmem_limit_bytes=VMEM_LIMIT),
        name="wkv_scan",
    )(r, k, v, kk, lw_f, a_f, r, k, v, kk, lw_b, a_b, k_a.reshape(1, d))


def _mix_post_kernel(yf_ref, yb_ref, g_ref, bonus_ref, pc_ref, pcp_ref, pcn_ref, pg_ref, x_ref, mod_ref,
                     lnxw_ref, lnxb_ref, convw_ref, convb_ref, cnw_ref, cnb_ref, cpw_ref, cpb_ref,
                     gateb_ref, wo_ref, ln1w_ref, ln1b_ref, seg_ref, segt_ref,
                     x1_ref, u2_ref, u2b_ref, hext_ref, *, blocks_per_seq, tb, alpha):
    d = D_MODEL
    i = pl.program_id(0)
    keep_prev = jnp.where(i % blocks_per_seq == 0, 0.0, 1.0)
    keep_next = jnp.where(i % blocks_per_seq == blocks_per_seq - 1, 0.0, 1.0)
    glu = lambda p: p[:, :d] * _sigmoid(p[:, d:])

    hext_ref[0:CONV_HALO, :] = glu(pcp_ref[...]) * keep_prev
    hext_ref[CONV_HALO:CONV_HALO + tb, :] = glu(pc_ref[...])
    hext_ref[CONV_HALO + tb:2 * CONV_HALO + tb, :] = glu(pcn_ref[...]) * keep_next
    acc = jnp.zeros((tb, d), F32) + convb_ref[...]
    first_off = CONV_HALO - CONV_PAD
    for res in range(8):
        part = None
        for j in range(CONV_WIDTH):
            if (j + first_off) % 8 != res:
                continue
            term = convw_ref[j:j + 1, :] * hext_ref[pl.ds(j + first_off - res, tb + 8), :]
            part = term if part is None else part + term
        acc = acc + part[res:res + tb]
    hc = _silu(_normalize(acc) * cnw_ref[...] + cnb_ref[...])
    y_conv = _mm(hc, cpw_ref[...]) + cpb_ref[...]

    seg, segt = seg_ref[...], segt_ref[...]
    y = yf_ref[...] + yb_ref[...]
    mu = _head_sum(y, seg, segt) * (1.0 / HEAD)
    yc = y - mu
    var = _head_sum(yc * yc, seg, segt) * (1.0 / HEAD)
    y_rwkv = (yc * lax.rsqrt(var + GN_EPS) * lnxw_ref[...] + lnxb_ref[...] + bonus_ref[...]) * g_ref[...]

    gates = _sigmoid(pg_ref[...] + gateb_ref[...])
    mix = _mm(gates[:, :d] * y_rwkv + gates[:, d:] * y_conv, wo_ref[...])
    m = mod_ref[0]
    x1 = _normalize(alpha * x_ref[...] + (1.0 + m[2:3]) * mix) * ln1w_ref[...] + ln1b_ref[...]
    x1_ref[...] = x1
    u2 = _normalize(x1) * (1.0 + m[4:5]) + m[3:4]
    u2_ref[...] = u2
    u2b_ref[...] = u2.astype(BF16)


def _mix_post(y_f, y_b, g, bonus, p_conv, p_gate, x2, mod, seq, alpha, lnx_w, lnx_b, conv_w, conv_b,
              cnorm_w, cnorm_b, cpw_w, cpw_b, gate_b, w_o, ln1_w, ln1_b):
    t, d = x2.shape
    tb = 256
    hb = CONV_HALO
    seg, segt = _head_indicator()
    row = lambda x: x.reshape(1, -1)
    const = lambda shape: pl.BlockSpec(shape, lambda i: (0,) * len(shape))
    tok = pl.BlockSpec((tb, d), lambda i: (i, 0))
    tok2 = pl.BlockSpec((tb, 2 * d), lambda i: (i, 0))
    return pl.pallas_call(
        functools.partial(_mix_post_kernel, blocks_per_seq=seq // tb, tb=tb, alpha=alpha),
        out_shape=(jax.ShapeDtypeStruct((t, d), F32), jax.ShapeDtypeStruct((t, d), F32),
                   jax.ShapeDtypeStruct((t, d), BF16)),
        grid=(t // tb,),
        in_specs=[tok, tok, tok, tok, tok2,
                  pl.BlockSpec((hb, 2 * d), lambda i: (jnp.maximum(i * (tb // hb) - 1, 0), 0)),
                  pl.BlockSpec((hb, 2 * d), lambda i: (jnp.minimum((i + 1) * (tb // hb), t // hb - 1), 0)),
                  tok2, tok,
                  pl.BlockSpec((1, 6, d), lambda i: (i // (seq // tb), 0, 0)),
                  const((1, d)), const((1, d)), const((CONV_WIDTH, d)), const((1, d)), const((1, d)),
                  const((1, d)), const((d, d)), const((1, d)), const((1, 2 * d)), const((d, d)),
                  const((1, d)), const((1, d)), const((d, 128)), const((128, d))],
        out_specs=(tok, tok, tok),
        scratch_shapes=[pltpu.VMEM((tb + 2 * CONV_HALO, d), F32)],
        compiler_params=pltpu.CompilerParams(dimension_semantics=("parallel",),
                                             vmem_limit_bytes=VMEM_LIMIT),
        name="mix_post",
    )(y_f, y_b, g, bonus, p_conv, p_conv, p_conv, p_gate, x2, mod,
      row(lnx_w), row(lnx_b), conv_w, row(conv_b), row(cnorm_w), row(cnorm_b), cpw_w.astype(BF16),
      row(cpw_b), row(gate_b), w_o.astype(BF16), row(ln1_w), row(ln1_b), seg, segt)


def _router_kernel(u_ref, rwt_ref, rb_ref, gate_ref, *, tb):
    neg = -jnp.inf
    logits = lax.dot_general(rwt_ref[...], u_ref[...], (((1,), (1,)), ((), ())),
                             precision=HIGHEST, preferred_element_type=F32)
    scores = _sigmoid(logits)
    sel = scores + rb_ref[...]
    sub = lax.broadcasted_iota(jnp.int32, (GROUP_SIZE, tb), 0)
    groups = [sel[GROUP_SIZE * g:GROUP_SIZE * (g + 1), :] for g in range(N_GROUPS)]

    gscore = []
    for xg in groups:
        m1 = jnp.max(xg, axis=0, keepdims=True)
        first = jnp.min(jnp.where(xg == m1, sub, GROUP_SIZE), axis=0, keepdims=True)
        m2 = jnp.max(jnp.where(sub == first, neg, xg), axis=0, keepdims=True)
        gscore.append(m1 + m2)
    cur = []
    for g, xg in enumerate(groups):
        rank = jnp.zeros((1, tb), jnp.int32)
        for g2 in range(N_GROUPS):
            if g2 == g:
                continue
            ahead = gscore[g2] > gscore[g]
            if g2 < g:
                ahead = ahead | (gscore[g2] == gscore[g])
            rank = rank + ahead.astype(jnp.int32)
        cur.append(jnp.where(rank < TOPK_GROUPS, xg, neg))
    chosen = [jnp.zeros((GROUP_SIZE, tb), jnp.bool_) for _ in range(N_GROUPS)]
    for _ in range(TOP_K):
        best = cur[0]
        for xg in cur[1:]:
            best = jnp.maximum(best, xg)
        best = jnp.max(best, axis=0, keepdims=True)
        cand = jnp.full((GROUP_SIZE, tb), N_EXPERTS, jnp.int32)
        for g, xg in enumerate(cur):
            cand = jnp.minimum(cand, jnp.where(xg == best, sub + GROUP_SIZE * g, N_EXPERTS))
        pick = jnp.min(cand, axis=0, keepdims=True)
        for g in range(N_GROUPS):
            hit = (sub + GROUP_SIZE * g) == pick
            chosen[g] = chosen[g] | hit
            cur[g] = jnp.where(hit, neg, cur[g])
    picked = jnp.concatenate(
        [jnp.where(chosen[g], scores[GROUP_SIZE * g:GROUP_SIZE * (g + 1), :], 0.0) for g in range(N_GROUPS)], 0)
    gate_t = picked / jnp.sum(picked, axis=0, keepdims=True) * ROUTED_SCALE

    ident = (lax.broadcasted_iota(jnp.int32, (tb, tb), 0)
             == lax.broadcasted_iota(jnp.int32, (tb, tb), 1)).astype(BF16)
    rest = gate_t
    out = jnp.zeros((tb, N_EXPERTS), F32)
    for _ in range(3):
        part = rest.astype(BF16)
        rest = rest - part.astype(F32)
        out = out + lax.dot_general(ident, part, (((1,), (1,)), ((), ())), preferred_element_type=F32)
    gate_ref[...] = out


def _router(u2, router_w, router_b):
    t, d = u2.shape
    tb = 512
    return pl.pallas_call(
        functools.partial(_router_kernel, tb=tb),
        out_shape=jax.ShapeDtypeStruct((t, N_EXPERTS), F32),
        grid=(t // tb,),
        in_specs=[pl.BlockSpec((tb, d), lambda i: (i, 0)),
                  pl.BlockSpec((N_EXPERTS, d), lambda i: (0, 0)),
                  pl.BlockSpec((N_EXPERTS, 1), lambda i: (0, 0))],
        out_specs=pl.BlockSpec((tb, N_EXPERTS), lambda i: (i, 0)),
        compiler_params=pltpu.CompilerParams(dimension_semantics=("parallel",),
                                             vmem_limit_bytes=VMEM_LIMIT),
        name="router",
    )(u2, router_w.T, router_b.reshape(N_EXPERTS, 1))


def _moe_kernel(u_ref, gate_ref, x1_ref, mod_ref, wg_ref, wu_ref, wd_ref, shg_ref, shu_ref, shd_ref,
                ln2w_ref, ln2b_ref, out_ref, acc_ref, *, alpha, n_steps):
    e = pl.program_id(1)
    ub = u_ref[...]

    @pl.when(e == 0)
    def _():
        hs = _silu(jnp.dot(ub, shg_ref[...], preferred_element_type=F32)) * jnp.dot(
            ub, shu_ref[...], preferred_element_type=F32)
        acc_ref[...] = _mm(hs, shd_ref[...])

    pick = (lax.broadcasted_iota(jnp.int32, (N_EXPERTS, 128), 0)
            == lax.broadcasted_iota(jnp.int32, (N_EXPERTS, 128), 1) + e * MOE_EXPERTS_PER_STEP)
    gate = _mm_split(gate_ref[...], pick.astype(BF16))
    acc = acc_ref[...]
    for j in range(MOE_EXPERTS_PER_STEP):
        h = _silu(jnp.dot(ub, wg_ref[j], preferred_element_type=F32)) * jnp.dot(
            ub, wu_ref[j], preferred_element_type=F32)
        acc = acc + _mm(h * gate[:, j:j + 1], wd_ref[j])
    acc_ref[...] = acc

    @pl.when(e == n_steps - 1)
    def _():
        m = mod_ref[0]
        out_ref[...] = (_normalize(alpha * x1_ref[...] + (1.0 + m[5:6]) * acc_ref[...]) * ln2w_ref[...]
                        + ln2b_ref[...])


def _moe(u2, gate, x1, mod, seq, alpha, exp_gate, exp_up, exp_down, sh_gate, sh_up, sh_down, ln2_w, ln2_b):
    t, d = u2.shape
    tb = 1024
    eps = MOE_EXPERTS_PER_STEP
    n_steps = N_EXPERTS // eps
    row = lambda x: x.reshape(1, -1)
    const = lambda shape: pl.BlockSpec(shape, lambda i, e: (0,) * len(shape))
    tok = pl.BlockSpec((tb, d), lambda i, e: (i, 0))
    return pl.pallas_call(
        functools.partial(_moe_kernel, alpha=alpha, n_steps=n_steps),
        out_shape=jax.ShapeDtypeStruct((t, d), F32),
        grid=(t // tb, n_steps),
        in_specs=[tok,
                  pl.BlockSpec((tb, N_EXPERTS), lambda i, e: (i, 0)),
                  tok,
                  pl.BlockSpec((1, 6, d), lambda i, e: (i // (seq // tb), 0, 0)),
                  pl.BlockSpec((eps, d, D_EXPERT), lambda i, e: (e, 0, 0)),
                  pl.BlockSpec((eps, d, D_EXPERT), lambda i, e: (e, 0, 0)),
                  pl.BlockSpec((eps, D_EXPERT, d), lambda i, e: (e, 0, 0)),
                  const(sh_gate.shape), const(sh_up.shape), const(sh_down.shape),
                  const((1, d)), const((1, d))],
        out_specs=tok,
        scratch_shapes=[pltpu.VMEM((tb, d), F32)],
        compiler_params=pltpu.CompilerParams(dimension_semantics=("parallel", "arbitrary"),
                                             vmem_limit_bytes=VMEM_LIMIT),
        name="moe",
    )(u2, gate, x1, mod, exp_gate.astype(BF16), exp_up.astype(BF16), exp_down.astype(BF16),
      sh_gate.astype(BF16), sh_up.astype(BF16), sh_down.astype(BF16), row(ln2_w), row(ln2_b))


def kernel(x, c, ada_w, ada_b, w_in, shift_mu, w0, w_up, a0, a_up, g_up, k_k, k_a, r_k, lnx_w, lnx_b,
           conv_w, conv_b, cnorm_w, cnorm_b, cpw_w, cpw_b, gate_b, w_o, ln1_w, ln1_b, router_w, router_b,
           exp_gate, exp_up, exp_down, sh_gate, sh_up, sh_down, ln2_w, ln2_b):
    batch, seq, d = x.shape
    depth = ada_w.shape[0]
    alpha = (2.0 * depth) ** 0.25
    x2 = x.reshape(batch * seq, d)
    for l in range(depth):
        mod = _adaln(c, ada_w[l], ada_b[l])
        p_rw, p_conv, p_gate = _in_proj(x2, mod, w_in[l], seq)
        r, k, v, kk, lw_f, lw_b, a_f, a_b, g, bonus = _rwkv_prep(
            p_rw, seq, shift_mu[l], w0[l], w_up[l], a0[l], a_up[l], g_up[l], k_k[l], k_a[l], r_k[l])
        y_f, y_b = _wkv_scan(r, k, v, kk, lw_f, lw_b, a_f, a_b, k_a[l], batch, seq)
        x1, u2, u2b = _mix_post(y_f, y_b, g, bonus, p_conv, p_gate, x2, mod, seq, alpha, lnx_w[l], lnx_b[l],
                           conv_w[l], conv_b[l], cnorm_w[l], cnorm_b[l], cpw_w[l], cpw_b[l], gate_b[l],
                           w_o[l], ln1_w[l], ln1_b[l])
        gate = _router(u2, router_w[l], router_b[l])
        x2 = _moe(u2b, gate, x1, mod, seq, alpha, exp_gate[l], exp_up[l], exp_down[l], sh_gate[l],
                  sh_up[l], sh_down[l], ln2_w[l], ln2_b[l])
    return x2.reshape(batch, seq, d)
```

```python
import functools
import math

import jax
import jax.numpy as jnp
from jax import lax
from jax.experimental import pallas as pl
from jax.experimental.pallas import tpu as pltpu

F32 = jnp.float32
BF16 = jnp.bfloat16
HIGHEST = lax.Precision.HIGHEST

D_MODEL = 1024
HEAD = 64
HEADS = D_MODEL // HEAD
LORA_G = 128
RW_COLS = 3 * D_MODEL + 4 * 64 + LORA_G
DECAY_SCALE = math.exp(-0.5)
GN_EPS = 64e-5
LN_EPS = 1e-5
CONV_WIDTH = 31
CONV_PAD = CONV_WIDTH // 2
CONV_HALO = 16
N_EXPERTS = 64
TOP_K = 8
N_GROUPS = 8
GROUP_SIZE = N_EXPERTS // N_GROUPS
TOPK_GROUPS = 4
D_EXPERT = 256
ROUTED_SCALE = 2.5

SCAN_HG = 2
SCAN_W = SCAN_HG * HEAD
SCAN_CHUNK = 64
SCAN_R = SCAN_HG * SCAN_CHUNK
SCAN_GROUPS = 8
SCAN_BLOCK = 256

ROUTER_TB = 4096
ROUTER_PIECE = 512
MOE_EXPERTS_PER_STEP = 4
VMEM_LIMIT = 56 * 1024 * 1024


def _mm(a, b):
    return jnp.dot(a.astype(BF16), b.astype(BF16), preferred_element_type=F32)


def _mm_nt(a, b):
    return lax.dot_general(a.astype(BF16), b.astype(BF16), (((1,), (1,)), ((), ())),
                           preferred_element_type=F32)


def _mm_tn(a, b):
    return lax.dot_general(a.astype(BF16), b.astype(BF16), (((0,), (0,)), ((), ())),
                           preferred_element_type=F32)


def _mm_f32(a, b):
    return jnp.dot(a, b, precision=HIGHEST, preferred_element_type=F32)


def _normalize(x):
    mu = jnp.mean(x, axis=-1, keepdims=True)
    xc = x - mu
    var = jnp.mean(xc * xc, axis=-1, keepdims=True)
    return xc * lax.rsqrt(var + LN_EPS)


def _sigmoid(x):
    return jax.nn.sigmoid(x)


def _silu(x):
    return x * jax.nn.sigmoid(x)


def _split_bf16(x):
    hi = x.astype(BF16)
    return hi, (x - hi.astype(F32)).astype(BF16)


def _mm_split(x, w01):
    hi, lo = _split_bf16(x)
    return (jnp.dot(hi, w01, preferred_element_type=F32) + jnp.dot(lo, w01, preferred_element_type=F32))


def _head_sum(x, seg, segt):
    return _mm_split(_mm_split(x, seg), segt)


def _adaln_kernel(c_ref, w_ref, b_ref, o_ref):
    o_ref[...] = _mm_f32(_silu(c_ref[...]), w_ref[...]) + b_ref[...]


def _adaln(c, ada_w, ada_b):
    b, d = c.shape
    n = ada_w.shape[1]
    rows = 8
    cp = jnp.zeros((rows, d), F32).at[:b].set(c)
    tn = 1536
    out = pl.pallas_call(
        _adaln_kernel,
        out_shape=jax.ShapeDtypeStruct((rows, n), F32),
        grid=(n // tn,),
        in_specs=[pl.BlockSpec((rows, d), lambda j: (0, 0)),
                  pl.BlockSpec((d, tn), lambda j: (0, j)),
                  pl.BlockSpec((1, tn), lambda j: (0, j))],
        out_specs=pl.BlockSpec((rows, tn), lambda j: (0, j)),
        name="adaln",
    )(cp, ada_w, ada_b.reshape(1, n))
    return out[:b].reshape(b, 6, d)


def _in_proj_kernel(x_ref, mod_ref, wrw_ref, wcv_ref, wgt_ref, prw_ref, pcv_ref, pgt_ref):
    m = mod_ref[0]
    u = (_normalize(x_ref[...]) * (1.0 + m[1:2]) + m[0:1]).astype(BF16)
    prw_ref[...] = jnp.dot(u, wrw_ref[...], preferred_element_type=F32)
    pcv_ref[...] = jnp.dot(u, wcv_ref[...], preferred_element_type=F32)
    pgt_ref[...] = jnp.dot(u, wgt_ref[...], preferred_element_type=F32)


def _in_proj(x2, mod, w_in, seq):
    t, d = x2.shape
    tm = 512
    w = w_in.astype(BF16)
    w_rw, w_cv, w_gt = w[:, :RW_COLS], w[:, RW_COLS:RW_COLS + 2 * d], w[:, RW_COLS + 2 * d:]
    resident = lambda shape: pl.BlockSpec(shape, lambda i: (0, 0), pipeline_mode=pl.Buffered(1))
    return pl.pallas_call(
        _in_proj_kernel,
        out_shape=(jax.ShapeDtypeStruct((t, RW_COLS), F32),
                   jax.ShapeDtypeStruct((t, 2 * d), F32),
                   jax.ShapeDtypeStruct((t, 2 * d), F32)),
        grid=(t // tm,),
        in_specs=[pl.BlockSpec((tm, d), lambda i: (i, 0)),
                  pl.BlockSpec((1, 6, d), lambda i: (i // (seq // tm), 0, 0)),
                  resident(w_rw.shape), resident(w_cv.shape), resident(w_gt.shape)],
        out_specs=(pl.BlockSpec((tm, RW_COLS), lambda i: (i, 0)),
                   pl.BlockSpec((tm, 2 * d), lambda i: (i, 0)),
                   pl.BlockSpec((tm, 2 * d), lambda i: (i, 0))),
        compiler_params=pltpu.CompilerParams(dimension_semantics=("parallel",),
                                             vmem_limit_bytes=VMEM_LIMIT),
        name="in_proj",
    )(x2, mod, w_rw, w_cv, w_gt)


def _rwkv_prep_kernel(p_ref, pp_ref, pn_ref, mu_ref, w0_ref, wup_ref, a0_ref, aup_ref, gup_ref,
                      kk_ref, ka_ref, rk_ref, seg_ref, segt_ref,
                      r_out, k_out, v_out, kk_out, lwf_out, lwb_out, af_out, ab_out, g_out, bonus_out,
                      *, blocks_per_seq, tb):
    d = D_MODEL
    i = pl.program_id(0)
    keep_prev = jnp.where(i % blocks_per_seq == 0, 0.0, 1.0)
    keep_next = jnp.where(i % blocks_per_seq == blocks_per_seq - 1, 0.0, 1.0)
    p = p_ref[...]
    prev_row = pp_ref[7:8, :] * keep_prev
    next_row = pn_ref[0:1, :] * keep_next
    sub = lax.broadcasted_iota(jnp.int32, (8, 1), 0)
    p_prev = pltpu.roll(p, 1, 0)
    p_prev = jnp.concatenate([jnp.where(sub == 0, prev_row, p_prev[:8]), p_prev[8:]], 0)
    p_next = pltpu.roll(p, tb - 1, 0)
    p_next = jnp.concatenate([p_next[:tb - 8], jnp.where(sub == 7, next_row, p_next[tb - 8:])], 0)
    mu_prev, mu_next = mu_ref[0:1, :], mu_ref[1:2, :]
    ps = (1.0 - mu_prev - mu_next) * p + mu_prev * p_prev + mu_next * p_next

    r, k, v = ps[:, :d], ps[:, d:2 * d], ps[:, 2 * d:3 * d]
    wl = ps[:, 3 * d:3 * d + 128]
    al = ps[:, 3 * d + 128:3 * d + 256]
    gl = ps[:, 3 * d + 256:3 * d + 384]
    seg, segt = seg_ref[...], segt_ref[...]

    lw = -DECAY_SCALE * _sigmoid(w0_ref[...] + _mm(jnp.tanh(wl), wup_ref[...]))
    a = _sigmoid(a0_ref[...] + _mm(al, aup_ref[...]))
    g = _mm(_sigmoid(gl), gup_ref[...])
    kraw = k * kk_ref[...]
    knorm = jnp.sqrt(_head_sum(kraw * kraw, seg, segt))
    kk = kraw / jnp.maximum(knorm, 1e-12)
    a_f = a[:, :d]
    k_f = k * (1.0 + (a_f - 1.0) * ka_ref[...])
    bonus = _head_sum(r * k_f * rk_ref[...], seg, segt) * v

    r_out[...] = r
    k_out[...] = k
    v_out[...] = v
    kk_out[...] = kk
    lwf_out[...] = lw[:, :d]
    lwb_out[...] = lw[:, d:]
    af_out[...] = a_f
    ab_out[...] = a[:, d:]
    g_out[...] = g
    bonus_out[...] = bonus


def _block_diag2(a, b):
    za = jnp.zeros((a.shape[0], b.shape[1]), a.dtype)
    zb = jnp.zeros((b.shape[0], a.shape[1]), a.dtype)
    return jnp.concatenate([jnp.concatenate([a, za], 1), jnp.concatenate([zb, b], 1)], 0)


def _head_indicator():
    ch = jnp.arange(D_MODEL) // HEAD
    seg = (ch[:, None] == jnp.arange(128)[None, :]).astype(BF16)
    return seg, seg.T


def _rwkv_prep(p_rw, seq, shift_mu, w0, w_up, a0, a_up, g_up, k_k, k_a, r_k):
    t = p_rw.shape[0]
    d = D_MODEL
    tb = 256
    hb = 8
    seg, segt = _head_indicator()
    wup = _block_diag2(w_up[0], w_up[1]).astype(BF16)
    aup = _block_diag2(a_up[0], a_up[1]).astype(BF16)
    row = lambda x: x.reshape(1, -1)
    const = lambda shape: pl.BlockSpec(shape, lambda i: (0,) * len(shape))
    tok = pl.BlockSpec((tb, d), lambda i: (i, 0))
    outs = tuple(jax.ShapeDtypeStruct((t, d), F32) for _ in range(10))
    return pl.pallas_call(
        functools.partial(_rwkv_prep_kernel, blocks_per_seq=seq // tb, tb=tb),
        out_shape=outs,
        grid=(t // tb,),
        in_specs=[pl.BlockSpec((tb, RW_COLS), lambda i: (i, 0)),
                  pl.BlockSpec((hb, RW_COLS), lambda i: (jnp.maximum(i * (tb // hb) - 1, 0), 0)),
                  pl.BlockSpec((hb, RW_COLS), lambda i: (jnp.minimum((i + 1) * (tb // hb), t // hb - 1), 0)),
                  const((2, RW_COLS)), const((1, 2 * d)), const((128, 2 * d)), const((1, 2 * d)),
                  const((128, 2 * d)), const((LORA_G, d)), const((1, d)), const((1, d)), const((1, d)),
                  const((d, 128)), const((128, d))],
        out_specs=tuple(tok for _ in range(10)),
        compiler_params=pltpu.CompilerParams(dimension_semantics=("parallel",),
                                             vmem_limit_bytes=VMEM_LIMIT),
        name="rwkv_prep",
    )(p_rw, p_rw, p_rw, shift_mu, row(w0), wup, row(a0), aup, g_up.astype(BF16), row(k_k), row(k_a),
      row(r_k), seg, segt)


def _scan_chunk(forward, r, k, v, kk, lw, a, ka, m, cm):
    c_steps, r_rows, w = SCAN_CHUNK, SCAN_R, SCAN_W
    strict = cm["strict_f"] if forward else cm["strict_b"]
    read_mask = cm["incl_f"] if forward else cm["strict_b"]
    same = cm["same"]
    c = lw
    step = 1
    while step < c_steps:
        if forward:
            c = c + jnp.where(cm["t_idx"] >= step, pltpu.roll(c, step, 0), 0.0)
        else:
            c = c + jnp.where(cm["t_idx"] < c_steps - step, pltpu.roll(c, c_steps - step, 0), 0.0)
        step *= 2
    ctot = c[c_steps - 1:c_steps, :] if forward else c[0:1, :]
    p_incl = jnp.exp(c)
    p_excl = jnp.exp(c - lw)
    p_inv = jnp.exp(-c)
    p_tail = jnp.exp(ctot - c)
    b = a * kk
    kd = k * (1.0 + (a - 1.0) * ka)
    kq = kk * p_excl
    rq = r * (p_incl if forward else p_excl)

    tile = lambda x: jnp.concatenate([x] * SCAN_HG, axis=0)
    expand = lambda x: jnp.where(same, tile(x), 0.0)
    fold = lambda x: sum(x[h * c_steps:(h + 1) * c_steps] for h in range(1, SCAN_HG)) + x[0:c_steps]

    kq_e, rq_e, v_e = expand(kq), expand(rq), expand(v)
    sc = _mm_nt(jnp.concatenate([kq_e, rq_e], 0),
                jnp.concatenate([tile(b * p_inv), tile(kd * p_inv)], 0))
    yield
    low = jnp.where(strict, sc[:r_rows, :r_rows], 0.0)
    akk = jnp.where(strict, sc[:r_rows, r_rows:], 0.0)
    arb = jnp.where(read_mask, sc[r_rows:, :r_rows], 0.0)
    ark = jnp.where(read_mask, sc[r_rows:, r_rows:], 0.0)

    eye = cm["eye"]
    ld = jnp.where(cm["m8"], low, 0.0)
    t_inv = eye - ld
    l2 = _mm(ld, ld)
    on_v = _mm(jnp.concatenate([akk.astype(BF16), ark.astype(BF16), expand(kd * p_tail).astype(BF16).T], 0), v_e)
    akk_v, ark_v, kkp_v = on_v[:r_rows], on_v[r_rows:2 * r_rows], on_v[2 * r_rows:]
    yield
    on_l2 = _mm(jnp.concatenate([t_inv, l2], 0), l2)
    t_inv = t_inv + on_l2[:r_rows]
    l4 = on_l2[r_rows:]
    yield
    t_inv = t_inv + _mm(t_inv, l4)
    yield
    for off in cm["offs"]:
        t_lo = _mm(t_inv, jnp.where(off, low, 0.0))
        yield
        t_inv = t_inv - _mm(t_lo, t_inv)
        yield

    z = _mm(t_inv, jnp.concatenate([kq_e, akk_v], 1))
    yield
    kqp_ut = jnp.concatenate([z[:, :w], -z[:, w:]], 1)
    on_ku = _mm(jnp.concatenate([arb.astype(BF16), expand(b * p_tail).astype(BF16).T], 0), kqp_ut)
    q2, g1 = on_ku[:r_rows], on_ku[r_rows:]
    yield
    rqp = fold(rq_e - q2[:, :w])
    y0 = fold(q2[:, w:] + ark_v)
    a_mat = jnp.where(eye > 0.5, jnp.exp(ctot), 0.0) - g1[:, :w]
    b_mat = g1[:, w:] + kkp_v
    on_m = _mm(jnp.concatenate([rqp, a_mat], 0), m)
    return on_m[:c_steps] + y0, on_m[c_steps:] + b_mat


def _interleave(gens):
    results = [None] * len(gens)
    active = list(range(len(gens)))
    while active:
        for i in list(active):
            try:
                next(gens[i])
            except StopIteration as done:
                results[i] = done.value
                active.remove(i)
    return results


def _scan_kernel(rf, kf, vf, kkf, lwf, af, rb, kb, vb, kkb, lwb, ab, ka_ref, yf_ref, yb_ref,
                 mf_ref, mb_ref, *, n_chunks):
    @pl.when(pl.program_id(2) == 0)
    def _():
        mf_ref[...] = jnp.zeros_like(mf_ref)
        mb_ref[...] = jnp.zeros_like(mb_ref)

    c_steps, r_rows = SCAN_CHUNK, SCAN_R
    row = lax.broadcasted_iota(jnp.int32, (r_rows, r_rows), 0)
    col = lax.broadcasted_iota(jnp.int32, (r_rows, r_rows), 1)
    same = (row // c_steps) == (col // c_steps)
    cm = {
        "same": same,
        "strict_f": same & (col < row), "incl_f": same & (col <= row), "strict_b": same & (col > row),
        "m8": (row // 8) == (col // 8),
        "offs": [((row // (2 * s)) == (col // (2 * s))) & ((row // s) != (col // s)) for s in (8, 16, 32)],
        "eye": (row == col).astype(F32),
        "t_idx": lax.broadcasted_iota(jnp.int32, (c_steps, SCAN_W), 0),
    }
    ka = ka_ref[...]

    def body(j, carry):
        of = pl.multiple_of(j * c_steps, c_steps)
        ob = pl.multiple_of((n_chunks - 1 - j) * c_steps, c_steps)
        sf, sb = pl.ds(of, c_steps), pl.ds(ob, c_steps)
        lanes = [slice(g * SCAN_W, (g + 1) * SCAN_W) for g in range(SCAN_GROUPS)]
        ins_f = [[ref[sf, ln] for ref in (rf, kf, vf, kkf, lwf, af)] + [ka[:, ln], mf_ref[g]]
                 for g, ln in enumerate(lanes)]
        ins_b = [[ref[sb, ln] for ref in (rb, kb, vb, kkb, lwb, ab)] + [ka[:, ln], mb_ref[g]]
                 for g, ln in enumerate(lanes)]
        outs = _interleave([_scan_chunk(True, *args, cm) for args in ins_f]
                           + [_scan_chunk(False, *args, cm) for args in ins_b])
        for g, ln in enumerate(lanes):
            yf_ref[sf, ln], mf_ref[g] = outs[g]
            yb_ref[sb, ln], mb_ref[g] = outs[SCAN_GROUPS + g]
        return carry

    lax.fori_loop(0, n_chunks, body, 0)


def _wkv_scan(r, k, v, kk, lw_f, lw_b, a_f, a_b, k_a, batch, seq):
    t, d = r.shape
    ns = seq // SCAN_BLOCK
    lanes = SCAN_GROUPS * SCAN_W
    fwd = pl.BlockSpec((SCAN_BLOCK, lanes), lambda b, g, s: (b * ns + s, g))
    bwd = pl.BlockSpec((SCAN_BLOCK, lanes), lambda b, g, s: (b * ns + ns - 1 - s, g))
    state = pltpu.VMEM((SCAN_GROUPS, SCAN_W, SCAN_W), F32)
    return pl.pallas_call(
        functools.partial(_scan_kernel, n_chunks=SCAN_BLOCK // SCAN_CHUNK),
        out_shape=(jax.ShapeDtypeStruct((t, d), F32), jax.ShapeDtypeStruct((t, d), F32)),
        grid=(batch, d // lanes, ns),
        in_specs=[fwd] * 6 + [bwd] * 6 + [pl.BlockSpec((1, lanes), lambda b, g, s: (0, g))],
        out_specs=(fwd, bwd),
        scratch_shapes=[state, state],
        compiler_params=pltpu.CompilerParams(dimension_semantics=("parallel", "parallel", "arbitrary"),
                                             vmem_limit_bytes=VMEM_LIMIT),
        name="wkv_scan",
    )(r, k, v, kk, lw_f, a_f, r, k, v, kk, lw_b, a_b, k_a.reshape(1, d))


def _mix_post_kernel(yf_ref, yb_ref, g_ref, bonus_ref, pc_ref, pcp_ref, pcn_ref, pg_ref, x_ref, mod_ref,
                     lnxw_ref, lnxb_ref, convw_ref, convb_ref, cnw_ref, cnb_ref, cpw_ref, cpb_ref,
                     gateb_ref, wo_ref, ln1w_ref, ln1b_ref, seg_ref, segt_ref,
                     x1_ref, u2hi_ref, u2lo_ref, hext_ref, *, blocks_per_seq, tb, alpha):
    d = D_MODEL
    i = pl.program_id(0)
    keep_prev = jnp.where(i % blocks_per_seq == 0, 0.0, 1.0)
    keep_next = jnp.where(i % blocks_per_seq == blocks_per_seq - 1, 0.0, 1.0)
    glu = lambda p: p[:, :d] * _sigmoid(p[:, d:])

    hext_ref[0:CONV_HALO, :] = glu(pcp_ref[...]) * keep_prev
    hext_ref[CONV_HALO:CONV_HALO + tb, :] = glu(pc_ref[...])
    hext_ref[CONV_HALO + tb:2 * CONV_HALO + tb, :] = glu(pcn_ref[...]) * keep_next
    acc = jnp.zeros((tb, d), F32) + convb_ref[...]
    first_off = CONV_HALO - CONV_PAD
    for res in range(8):
        part = None
        for j in range(CONV_WIDTH):
            if (j + first_off) % 8 != res:
                continue
            term = convw_ref[j:j + 1, :] * hext_ref[pl.ds(j + first_off - res, tb + 8), :]
            part = term if part is None else part + term
        acc = acc + part[res:res + tb]
    hc = _silu(_normalize(acc) * cnw_ref[...] + cnb_ref[...])
    y_conv = _mm(hc, cpw_ref[...]) + cpb_ref[...]

    seg, segt = seg_ref[...], segt_ref[...]
    y = yf_ref[...] + yb_ref[...]
    mu = _head_sum(y, seg, segt) * (1.0 / HEAD)
    yc = y - mu
    var = _head_sum(yc * yc, seg, segt) * (1.0 / HEAD)
    y_rwkv = (yc * lax.rsqrt(var + GN_EPS) * lnxw_ref[...] + lnxb_ref[...] + bonus_ref[...]) * g_ref[...]

    gates = _sigmoid(pg_ref[...] + gateb_ref[...])
    mix = _mm(gates[:, :d] * y_rwkv + gates[:, d:] * y_conv, wo_ref[...])
    m = mod_ref[0]
    x1 = _normalize(alpha * x_ref[...] + (1.0 + m[2:3]) * mix) * ln1w_ref[...] + ln1b_ref[...]
    x1_ref[...] = x1
    u2 = _normalize(x1) * (1.0 + m[4:5]) + m[3:4]
    u2hi_ref[...], u2lo_ref[...] = _split_bf16(u2)


def _mix_post(y_f, y_b, g, bonus, p_conv, p_gate, x2, mod, seq, alpha, lnx_w, lnx_b, conv_w, conv_b,
              cnorm_w, cnorm_b, cpw_w, cpw_b, gate_b, w_o, ln1_w, ln1_b):
    t, d = x2.shape
    tb = 256
    hb = CONV_HALO
    seg, segt = _head_indicator()
    row = lambda x: x.reshape(1, -1)
    const = lambda shape: pl.BlockSpec(shape, lambda i: (0,) * len(shape))
    tok = pl.BlockSpec((tb, d), lambda i: (i, 0))
    tok2 = pl.BlockSpec((tb, 2 * d), lambda i: (i, 0))
    return pl.pallas_call(
        functools.partial(_mix_post_kernel, blocks_per_seq=seq // tb, tb=tb, alpha=alpha),
        out_shape=(jax.ShapeDtypeStruct((t, d), F32), jax.ShapeDtypeStruct((t, d), BF16),
                   jax.ShapeDtypeStruct((t, d), BF16)),
        grid=(t // tb,),
        in_specs=[tok, tok, tok, tok, tok2,
                  pl.BlockSpec((hb, 2 * d), lambda i: (jnp.maximum(i * (tb // hb) - 1, 0), 0)),
                  pl.BlockSpec((hb, 2 * d), lambda i: (jnp.minimum((i + 1) * (tb // hb), t // hb - 1), 0)),
                  tok2, tok,
                  pl.BlockSpec((1, 6, d), lambda i: (i // (seq // tb), 0, 0)),
                  const((1, d)), const((1, d)), const((CONV_WIDTH, d)), const((1, d)), const((1, d)),
                  const((1, d)), const((d, d)), const((1, d)), const((1, 2 * d)), const((d, d)),
                  const((1, d)), const((1, d)), const((d, 128)), const((128, d))],
        out_specs=(tok, tok, tok),
        scratch_shapes=[pltpu.VMEM((tb + 2 * CONV_HALO, d), F32)],
        compiler_params=pltpu.CompilerParams(dimension_semantics=("parallel",),
                                             vmem_limit_bytes=VMEM_LIMIT),
        name="mix_post",
    )(y_f, y_b, g, bonus, p_conv, p_conv, p_conv, p_gate, x2, mod,
      row(lnx_w), row(lnx_b), conv_w, row(conv_b), row(cnorm_w), row(cnorm_b), cpw_w.astype(BF16),
      row(cpw_b), row(gate_b), w_o.astype(BF16), row(ln1_w), row(ln1_b), seg, segt)


def _router_kernel(uhi_ref, ulo_ref, rwhi_ref, rwlo_ref, rb_ref, gate_ref, *, tb):
    neg = -jnp.inf
    nt = lambda a, b: lax.dot_general(a, b, (((1,), (1,)), ((), ())), preferred_element_type=F32)
    u_hi = uhi_ref[...]
    logits = nt(rwhi_ref[...], u_hi) + (nt(rwhi_ref[...], ulo_ref[...]) + nt(rwlo_ref[...], u_hi))
    scores = _sigmoid(logits)
    sel = scores + rb_ref[...]
    sub = lax.broadcasted_iota(jnp.int32, (GROUP_SIZE, tb), 0)
    groups = [sel[GROUP_SIZE * g:GROUP_SIZE * (g + 1), :] for g in range(N_GROUPS)]

    gscore = []
    for xg in groups:
        m1 = jnp.max(xg, axis=0, keepdims=True)
        first = jnp.min(jnp.where(xg == m1, sub, GROUP_SIZE), axis=0, keepdims=True)
        m2 = jnp.max(jnp.where(sub == first, neg, xg), axis=0, keepdims=True)
        gscore.append(m1 + m2)
    cur = []
    for g, xg in enumerate(groups):
        rank = jnp.zeros((1, tb), jnp.int32)
        for g2 in range(N_GROUPS):
            if g2 == g:
                continue
            ahead = gscore[g2] > gscore[g]
            if g2 < g:
                ahead = ahead | (gscore[g2] == gscore[g])
            rank = rank + ahead.astype(jnp.int32)
        cur.append(jnp.where(rank < TOPK_GROUPS, xg, neg))
    chosen = [jnp.zeros((GROUP_SIZE, tb), jnp.bool_) for _ in range(N_GROUPS)]
    for _ in range(TOP_K):
        best = cur[0]
        for xg in cur[1:]:
            best = jnp.maximum(best, xg)
        best = jnp.max(best, axis=0, keepdims=True)
        cand = jnp.full((GROUP_SIZE, tb), N_EXPERTS, jnp.int32)
        for g, xg in enumerate(cur):
            cand = jnp.minimum(cand, jnp.where(xg == best, sub + GROUP_SIZE * g, N_EXPERTS))
        pick = jnp.min(cand, axis=0, keepdims=True)
        for g in range(N_GROUPS):
            hit = (sub + GROUP_SIZE * g) == pick
            chosen[g] = chosen[g] | hit
            cur[g] = jnp.where(hit, neg, cur[g])
    picked = jnp.concatenate(
        [jnp.where(chosen[g], scores[GROUP_SIZE * g:GROUP_SIZE * (g + 1), :], 0.0) for g in range(N_GROUPS)], 0)
    gate_t = picked / jnp.sum(picked, axis=0, keepdims=True) * ROUTED_SCALE

    pc = ROUTER_PIECE
    ident = (lax.broadcasted_iota(jnp.int32, (pc, pc), 0)
             == lax.broadcasted_iota(jnp.int32, (pc, pc), 1)).astype(BF16)
    for s in range(tb // pc):
        rest = gate_t[:, s * pc:(s + 1) * pc]
        out = jnp.zeros((pc, N_EXPERTS), F32)
        for _ in range(3):
            part = rest.astype(BF16)
            rest = rest - part.astype(F32)
            out = out + lax.dot_general(ident, part, (((1,), (1,)), ((), ())), preferred_element_type=F32)
        gate_ref[s * pc:(s + 1) * pc, :] = out


def _router(u2_hi, u2_lo, router_w, router_b):
    t, d = u2_hi.shape
    tb = ROUTER_TB
    rw_hi, rw_lo = _split_bf16(router_w.T)
    return pl.pallas_call(
        functools.partial(_router_kernel, tb=tb),
        out_shape=jax.ShapeDtypeStruct((t, N_EXPERTS), F32),
        grid=(t // tb,),
        in_specs=[pl.BlockSpec((tb, d), lambda i: (i, 0)),
                  pl.BlockSpec((tb, d), lambda i: (i, 0)),
                  pl.BlockSpec((N_EXPERTS, d), lambda i: (0, 0)),
                  pl.BlockSpec((N_EXPERTS, d), lambda i: (0, 0)),
                  pl.BlockSpec((N_EXPERTS, 1), lambda i: (0, 0))],
        out_specs=pl.BlockSpec((tb, N_EXPERTS), lambda i: (i, 0)),
        compiler_params=pltpu.CompilerParams(dimension_semantics=("parallel",),
                                             vmem_limit_bytes=VMEM_LIMIT),
        name="router",
    )(u2_hi, u2_lo, rw_hi, rw_lo, router_b.reshape(N_EXPERTS, 1))


def _moe_kernel(u_ref, gate_ref, x1_ref, mod_ref, wg_ref, wu_ref, wd_ref, shg_ref, shu_ref, shd_ref,
                ln2w_ref, ln2b_ref, out_ref, acc_ref, *, alpha, n_steps):
    e = pl.program_id(1)
    ub = u_ref[...]

    @pl.when(e == 0)
    def _():
        hs = _silu(jnp.dot(ub, shg_ref[...], preferred_element_type=F32)) * jnp.dot(
            ub, shu_ref[...], preferred_element_type=F32)
        acc_ref[...] = _mm(hs, shd_ref[...])

    pick = (lax.broadcasted_iota(jnp.int32, (N_EXPERTS, 128), 0)
            == lax.broadcasted_iota(jnp.int32, (N_EXPERTS, 128), 1) + e * MOE_EXPERTS_PER_STEP)
    gate = _mm_split(gate_ref[...], pick.astype(BF16))
    acc = acc_ref[...]
    for j in range(MOE_EXPERTS_PER_STEP):
        h = _silu(jnp.dot(ub, wg_ref[j], preferred_element_type=F32)) * jnp.dot(
            ub, wu_ref[j], preferred_element_type=F32)
        acc = acc + _mm(h * gate[:, j:j + 1], wd_ref[j])
    acc_ref[...] = acc

    @pl.when(e == n_steps - 1)
    def _():
        m = mod_ref[0]
        out_ref[...] = (_normalize(alpha * x1_ref[...] + (1.0 + m[5:6]) * acc_ref[...]) * ln2w_ref[...]
                        + ln2b_ref[...])


def _moe(u2, gate, x1, mod, seq, alpha, exp_gate, exp_up, exp_down, sh_gate, sh_up, sh_down, ln2_w, ln2_b):
    t, d = u2.shape
    tb = 1024
    eps = MOE_EXPERTS_PER_STEP
    n_steps = N_EXPERTS // eps
    row = lambda x: x.reshape(1, -1)
    const = lambda shape: pl.BlockSpec(shape, lambda i, e: (0,) * len(shape))
    tok = pl.BlockSpec((tb, d), lambda i, e: (i, 0))
    return pl.pallas_call(
        functools.partial(_moe_kernel, alpha=alpha, n_steps=n_steps),
        out_shape=jax.ShapeDtypeStruct((t, d), F32),
        grid=(t // tb, n_steps),
        in_specs=[tok,
                  pl.BlockSpec((tb, N_EXPERTS), lambda i, e: (i, 0)),
                  tok,
                  pl.BlockSpec((1, 6, d), lambda i, e: (i // (seq // tb), 0, 0)),
                  pl.BlockSpec((eps, d, D_EXPERT), lambda i, e: (e, 0, 0)),
                  pl.BlockSpec((eps, d, D_EXPERT), lambda i, e: (e, 0, 0)),
                  pl.BlockSpec((eps, D_EXPERT, d), lambda i, e: (e, 0, 0)),
                  const(sh_gate.shape), const(sh_up.shape), const(sh_down.shape),
                  const((1, d)), const((1, d))],
        out_specs=tok,
        scratch_shapes=[pltpu.VMEM((tb, d), F32)],
        compiler_params=pltpu.CompilerParams(dimension_semantics=("parallel", "arbitrary"),
                                             vmem_limit_bytes=VMEM_LIMIT),
        name="moe",
    )(u2, gate, x1, mod, exp_gate.astype(BF16), exp_up.astype(BF16), exp_down.astype(BF16),
      sh_gate.astype(BF16), sh_up.astype(BF16), sh_down.astype(BF16), row(ln2_w), row(ln2_b))


def kernel(x, c, ada_w, ada_b, w_in, shift_mu, w0, w_up, a0, a_up, g_up, k_k, k_a, r_k, lnx_w, lnx_b,
           conv_w, conv_b, cnorm_w, cnorm_b, cpw_w, cpw_b, gate_b, w_o, ln1_w, ln1_b, router_w, router_b,
           exp_gate, exp_up, exp_down, sh_gate, sh_up, sh_down, ln2_w, ln2_b):
    batch, seq, d = x.shape
    depth = ada_w.shape[0]
    alpha = (2.0 * depth) ** 0.25
    x2 = x.reshape(batch * seq, d)
    for l in range(depth):
        mod = _adaln(c, ada_w[l], ada_b[l])
        p_rw, p_conv, p_gate = _in_proj(x2, mod, w_in[l], seq)
        r, k, v, kk, lw_f, lw_b, a_f, a_b, g, bonus = _rwkv_prep(
            p_rw, seq, shift_mu[l], w0[l], w_up[l], a0[l], a_up[l], g_up[l], k_k[l], k_a[l], r_k[l])
        y_f, y_b = _wkv_scan(r, k, v, kk, lw_f, lw_b, a_f, a_b, k_a[l], batch, seq)
        x1, u2_hi, u2_lo = _mix_post(y_f, y_b, g, bonus, p_conv, p_gate, x2, mod, seq, alpha, lnx_w[l],
                                     lnx_b[l], conv_w[l], conv_b[l], cnorm_w[l], cnorm_b[l], cpw_w[l],
                                     cpw_b[l], gate_b[l], w_o[l], ln1_w[l], ln1_b[l])
        gate = _router(u2_hi, u2_lo, router_w[l], router_b[l])
        x2 = _moe(u2_hi, gate, x1, mod, seq, alpha, exp_gate[l], exp_up[l], exp_down[l], sh_gate[l],
                  sh_up[l], sh_down[l], ln2_w[l], ln2_b[l])
    return x2.reshape(batch, seq, d)
```

```python
import functools
import math

import jax
import jax.numpy as jnp
from jax import lax
from jax.experimental import pallas as pl
from jax.experimental.pallas import tpu as pltpu

F32 = jnp.float32
BF16 = jnp.bfloat16
HIGHEST = lax.Precision.HIGHEST

D_MODEL = 1024
HEAD = 64
HEADS = D_MODEL // HEAD
LORA_G = 128
RW_COLS = 3 * D_MODEL + 4 * 64 + LORA_G
DECAY_SCALE = math.exp(-0.5)
GN_EPS = 64e-5
LN_EPS = 1e-5
CONV_WIDTH = 31
CONV_PAD = CONV_WIDTH // 2
CONV_HALO = 16
N_EXPERTS = 64
TOP_K = 8
N_GROUPS = 8
GROUP_SIZE = N_EXPERTS // N_GROUPS
TOPK_GROUPS = 4
D_EXPERT = 256
ROUTED_SCALE = 2.5

SCAN_HG = 2
SCAN_W = SCAN_HG * HEAD
SCAN_CHUNK = 64
SCAN_R = SCAN_HG * SCAN_CHUNK
SCAN_GROUPS = 8
SCAN_BLOCK = 256

ROUTER_TB = 4096
ROUTER_PIECE = 512
MOE_EXPERTS_PER_STEP = 8
VMEM_LIMIT = 56 * 1024 * 1024
MOE_VMEM_LIMIT = 60 * 1024 * 1024


def _mm(a, b):
    return jnp.dot(a.astype(BF16), b.astype(BF16), preferred_element_type=F32)


def _mm_nt(a, b):
    return lax.dot_general(a.astype(BF16), b.astype(BF16), (((1,), (1,)), ((), ())),
                           preferred_element_type=F32)


def _mm_tn(a, b):
    return lax.dot_general(a.astype(BF16), b.astype(BF16), (((0,), (0,)), ((), ())),
                           preferred_element_type=F32)


def _mm_f32(a, b):
    return jnp.dot(a, b, precision=HIGHEST, preferred_element_type=F32)


def _normalize(x):
    mu = jnp.mean(x, axis=-1, keepdims=True)
    xc = x - mu
    var = jnp.mean(xc * xc, axis=-1, keepdims=True)
    return xc * lax.rsqrt(var + LN_EPS)


def _sigmoid(x):
    return jax.nn.sigmoid(x)


def _silu(x):
    return x * jax.nn.sigmoid(x)


def _split_bf16(x):
    hi = x.astype(BF16)
    return hi, (x - hi.astype(F32)).astype(BF16)


def _mm_split(x, w01):
    hi, lo = _split_bf16(x)
    return (jnp.dot(hi, w01, preferred_element_type=F32) + jnp.dot(lo, w01, preferred_element_type=F32))


def _head_sum(x, seg, segt):
    return _mm_split(_mm_split(x, seg), segt)


def _adaln_kernel(c_ref, w_ref, b_ref, o_ref):
    o_ref[...] = _mm_f32(_silu(c_ref[...]), w_ref[...]) + b_ref[...]


def _adaln(c, ada_w, ada_b):
    b, d = c.shape
    n = ada_w.shape[1]
    rows = 8
    cp = jnp.zeros((rows, d), F32).at[:b].set(c)
    tn = 1536
    out = pl.pallas_call(
        _adaln_kernel,
        out_shape=jax.ShapeDtypeStruct((rows, n), F32),
        grid=(n // tn,),
        in_specs=[pl.BlockSpec((rows, d), lambda j: (0, 0)),
                  pl.BlockSpec((d, tn), lambda j: (0, j)),
                  pl.BlockSpec((1, tn), lambda j: (0, j))],
        out_specs=pl.BlockSpec((rows, tn), lambda j: (0, j)),
        name="adaln",
    )(cp, ada_w, ada_b.reshape(1, n))
    return out[:b].reshape(b, 6, d)


def _in_proj_kernel(x_ref, mod_ref, wrw_ref, wcv_ref, wgt_ref, prw_ref, pcv_ref, pgt_ref):
    m = mod_ref[0]
    u = (_normalize(x_ref[...]) * (1.0 + m[1:2]) + m[0:1]).astype(BF16)
    prw_ref[...] = jnp.dot(u, wrw_ref[...], preferred_element_type=F32)
    pcv_ref[...] = jnp.dot(u, wcv_ref[...], preferred_element_type=F32)
    pgt_ref[...] = jnp.dot(u, wgt_ref[...], preferred_element_type=F32)


def _in_proj(x2, mod, w_in, seq):
    t, d = x2.shape
    tm = 512
    w = w_in.astype(BF16)
    w_rw, w_cv, w_gt = w[:, :RW_COLS], w[:, RW_COLS:RW_COLS + 2 * d], w[:, RW_COLS + 2 * d:]
    resident = lambda shape: pl.BlockSpec(shape, lambda i: (0, 0), pipeline_mode=pl.Buffered(1))
    return pl.pallas_call(
        _in_proj_kernel,
        out_shape=(jax.ShapeDtypeStruct((t, RW_COLS), F32),
                   jax.ShapeDtypeStruct((t, 2 * d), F32),
                   jax.ShapeDtypeStruct((t, 2 * d), F32)),
        grid=(t // tm,),
        in_specs=[pl.BlockSpec((tm, d), lambda i: (i, 0)),
                  pl.BlockSpec((1, 6, d), lambda i: (i // (seq // tm), 0, 0)),
                  resident(w_rw.shape), resident(w_cv.shape), resident(w_gt.shape)],
        out_specs=(pl.BlockSpec((tm, RW_COLS), lambda i: (i, 0)),
                   pl.BlockSpec((tm, 2 * d), lambda i: (i, 0)),
                   pl.BlockSpec((tm, 2 * d), lambda i: (i, 0))),
        compiler_params=pltpu.CompilerParams(dimension_semantics=("parallel",),
                                             vmem_limit_bytes=VMEM_LIMIT),
        name="in_proj",
    )(x2, mod, w_rw, w_cv, w_gt)


def _rwkv_prep_kernel(p_ref, pp_ref, pn_ref, mu_ref, w0_ref, wup_ref, a0_ref, aup_ref, gup_ref,
                      kk_ref, ka_ref, rk_ref, seg_ref, segt_ref,
                      r_out, k_out, v_out, kk_out, lwf_out, lwb_out, af_out, ab_out, g_out, bonus_out,
                      *, blocks_per_seq, tb):
    d = D_MODEL
    i = pl.program_id(0)
    keep_prev = jnp.where(i % blocks_per_seq == 0, 0.0, 1.0)
    keep_next = jnp.where(i % blocks_per_seq == blocks_per_seq - 1, 0.0, 1.0)
    p = p_ref[...]
    prev_row = pp_ref[7:8, :] * keep_prev
    next_row = pn_ref[0:1, :] * keep_next
    sub = lax.broadcasted_iota(jnp.int32, (8, 1), 0)
    p_prev = pltpu.roll(p, 1, 0)
    p_prev = jnp.concatenate([jnp.where(sub == 0, prev_row, p_prev[:8]), p_prev[8:]], 0)
    p_next = pltpu.roll(p, tb - 1, 0)
    p_next = jnp.concatenate([p_next[:tb - 8], jnp.where(sub == 7, next_row, p_next[tb - 8:])], 0)
    mu_prev, mu_next = mu_ref[0:1, :], mu_ref[1:2, :]
    ps = (1.0 - mu_prev - mu_next) * p + mu_prev * p_prev + mu_next * p_next

    r, k, v = ps[:, :d], ps[:, d:2 * d], ps[:, 2 * d:3 * d]
    wl = ps[:, 3 * d:3 * d + 128]
    al = ps[:, 3 * d + 128:3 * d + 256]
    gl = ps[:, 3 * d + 256:3 * d + 384]
    seg, segt = seg_ref[...], segt_ref[...]

    lw = -DECAY_SCALE * _sigmoid(w0_ref[...] + _mm(jnp.tanh(wl), wup_ref[...]))
    a = _sigmoid(a0_ref[...] + _mm(al, aup_ref[...]))
    g = _mm(_sigmoid(gl), gup_ref[...])
    kraw = k * kk_ref[...]
    knorm = jnp.sqrt(_head_sum(kraw * kraw, seg, segt))
    kk = kraw / jnp.maximum(knorm, 1e-12)
    a_f = a[:, :d]
    k_f = k * (1.0 + (a_f - 1.0) * ka_ref[...])
    bonus = _head_sum(r * k_f * rk_ref[...], seg, segt) * v

    r_out[...] = r
    k_out[...] = k
    v_out[...] = v
    kk_out[...] = kk
    lwf_out[...] = lw[:, :d]
    lwb_out[...] = lw[:, d:]
    af_out[...] = a_f
    ab_out[...] = a[:, d:]
    g_out[...] = g
    bonus_out[...] = bonus


def _block_diag2(a, b):
    za = jnp.zeros((a.shape[0], b.shape[1]), a.dtype)
    zb = jnp.zeros((b.shape[0], a.shape[1]), a.dtype)
    return jnp.concatenate([jnp.concatenate([a, za], 1), jnp.concatenate([zb, b], 1)], 0)


def _head_indicator():
    ch = jnp.arange(D_MODEL) // HEAD
    seg = (ch[:, None] == jnp.arange(128)[None, :]).astype(BF16)
    return seg, seg.T


def _rwkv_prep(p_rw, seq, shift_mu, w0, w_up, a0, a_up, g_up, k_k, k_a, r_k):
    t = p_rw.shape[0]
    d = D_MODEL
    tb = 256
    hb = 8
    seg, segt = _head_indicator()
    wup = _block_diag2(w_up[0], w_up[1]).astype(BF16)
    aup = _block_diag2(a_up[0], a_up[1]).astype(BF16)
    row = lambda x: x.reshape(1, -1)
    const = lambda shape: pl.BlockSpec(shape, lambda i: (0,) * len(shape))
    tok = pl.BlockSpec((tb, d), lambda i: (i, 0))
    outs = tuple(jax.ShapeDtypeStruct((t, d), F32) for _ in range(10))
    return pl.pallas_call(
        functools.partial(_rwkv_prep_kernel, blocks_per_seq=seq // tb, tb=tb),
        out_shape=outs,
        grid=(t // tb,),
        in_specs=[pl.BlockSpec((tb, RW_COLS), lambda i: (i, 0)),
                  pl.BlockSpec((hb, RW_COLS), lambda i: (jnp.maximum(i * (tb // hb) - 1, 0), 0)),
                  pl.BlockSpec((hb, RW_COLS), lambda i: (jnp.minimum((i + 1) * (tb // hb), t // hb - 1), 0)),
                  const((2, RW_COLS)), const((1, 2 * d)), const((128, 2 * d)), const((1, 2 * d)),
                  const((128, 2 * d)), const((LORA_G, d)), const((1, d)), const((1, d)), const((1, d)),
                  const((d, 128)), const((128, d))],
        out_specs=tuple(tok for _ in range(10)),
        compiler_params=pltpu.CompilerParams(dimension_semantics=("parallel",),
                                             vmem_limit_bytes=VMEM_LIMIT),
        name="rwkv_prep",
    )(p_rw, p_rw, p_rw, shift_mu, row(w0), wup, row(a0), aup, g_up.astype(BF16), row(k_k), row(k_a),
      row(r_k), seg, segt)


def _scan_chunk(forward, r, k, v, kk, lw, a, ka, m, cm):
    c_steps, r_rows, w = SCAN_CHUNK, SCAN_R, SCAN_W
    strict = cm["strict_f"] if forward else cm["strict_b"]
    read_mask = cm["incl_f"] if forward else cm["strict_b"]
    same = cm["same"]
    c = lw
    step = 1
    while step < c_steps:
        if forward:
            c = c + jnp.where(cm["t_idx"] >= step, pltpu.roll(c, step, 0), 0.0)
        else:
            c = c + jnp.where(cm["t_idx"] < c_steps - step, pltpu.roll(c, c_steps - step, 0), 0.0)
        step *= 2
    ctot = c[c_steps - 1:c_steps, :] if forward else c[0:1, :]
    p_incl = jnp.exp(c)
    p_excl = jnp.exp(c - lw)
    p_inv = jnp.exp(-c)
    p_tail = jnp.exp(ctot - c)
    b = a * kk
    kd = k * (1.0 + (a - 1.0) * ka)
    kq = kk * p_excl
    rq = r * (p_incl if forward else p_excl)

    tile = lambda x: jnp.concatenate([x] * SCAN_HG, axis=0)
    expand = lambda x: jnp.where(same, tile(x), 0.0)
    fold = lambda x: sum(x[h * c_steps:(h + 1) * c_steps] for h in range(1, SCAN_HG)) + x[0:c_steps]

    kq_e, rq_e, v_e = expand(kq), expand(rq), expand(v)
    sc = _mm_nt(jnp.concatenate([kq_e, rq_e], 0),
                jnp.concatenate([tile(b * p_inv), tile(kd * p_inv)], 0))
    yield
    low = jnp.where(strict, sc[:r_rows, :r_rows], 0.0)
    akk = jnp.where(strict, sc[:r_rows, r_rows:], 0.0)
    arb = jnp.where(read_mask, sc[r_rows:, :r_rows], 0.0)
    ark = jnp.where(read_mask, sc[r_rows:, r_rows:], 0.0)

    eye = cm["eye"]
    ld = jnp.where(cm["m8"], low, 0.0)
    t_inv = eye - ld
    l2 = _mm(ld, ld)
    on_v = _mm(jnp.concatenate([akk.astype(BF16), ark.astype(BF16), expand(kd * p_tail).astype(BF16).T], 0), v_e)
    akk_v, ark_v, kkp_v = on_v[:r_rows], on_v[r_rows:2 * r_rows], on_v[2 * r_rows:]
    yield
    on_l2 = _mm(jnp.concatenate([t_inv, l2], 0), l2)
    t_inv = t_inv + on_l2[:r_rows]
    l4 = on_l2[r_rows:]
    yield
    t_inv = t_inv + _mm(t_inv, l4)
    yield
    for off in cm["offs"]:
        t_lo = _mm(t_inv, jnp.where(off, low, 0.0))
        yield
        t_inv = t_inv - _mm(t_lo, t_inv)
        yield

    z = _mm(t_inv, jnp.concatenate([kq_e, akk_v], 1))
    yield
    kqp_ut = jnp.concatenate([z[:, :w], -z[:, w:]], 1)
    on_ku = _mm(jnp.concatenate([arb.astype(BF16), expand(b * p_tail).astype(BF16).T], 0), kqp_ut)
    q2, g1 = on_ku[:r_rows], on_ku[r_rows:]
    yield
    rqp = fold(rq_e - q2[:, :w])
    y0 = fold(q2[:, w:] + ark_v)
    a_mat = jnp.where(eye > 0.5, jnp.exp(ctot), 0.0) - g1[:, :w]
    b_mat = g1[:, w:] + kkp_v
    on_m = _mm(jnp.concatenate([rqp, a_mat], 0), m)
    return on_m[:c_steps] + y0, on_m[c_steps:] + b_mat


def _interleave(gens):
    results = [None] * len(gens)
    active = list(range(len(gens)))
    while active:
        for i in list(active):
            try:
                next(gens[i])
            except StopIteration as done:
                results[i] = done.value
                active.remove(i)
    return results


def _scan_kernel(rf, kf, vf, kkf, lwf, af, rb, kb, vb, kkb, lwb, ab, ka_ref, yf_ref, yb_ref,
                 mf_ref, mb_ref, *, n_chunks):
    @pl.when(pl.program_id(2) == 0)
    def _():
        mf_ref[...] = jnp.zeros_like(mf_ref)
        mb_ref[...] = jnp.zeros_like(mb_ref)

    c_steps, r_rows = SCAN_CHUNK, SCAN_R
    row = lax.broadcasted_iota(jnp.int32, (r_rows, r_rows), 0)
    col = lax.broadcasted_iota(jnp.int32, (r_rows, r_rows), 1)
    same = (row // c_steps) == (col // c_steps)
    cm = {
        "same": same,
        "strict_f": same & (col < row), "incl_f": same & (col <= row), "strict_b": same & (col > row),
        "m8": (row // 8) == (col // 8),
        "offs": [((row // (2 * s)) == (col // (2 * s))) & ((row // s) != (col // s)) for s in (8, 16, 32)],
        "eye": (row == col).astype(F32),
        "t_idx": lax.broadcasted_iota(jnp.int32, (c_steps, SCAN_W), 0),
    }
    ka = ka_ref[...]

    def body(j, carry):
        of = pl.multiple_of(j * c_steps, c_steps)
        ob = pl.multiple_of((n_chunks - 1 - j) * c_steps, c_steps)
        sf, sb = pl.ds(of, c_steps), pl.ds(ob, c_steps)
        lanes = [slice(g * SCAN_W, (g + 1) * SCAN_W) for g in range(SCAN_GROUPS)]
        ins_f = [[ref[sf, ln] for ref in (rf, kf, vf, kkf, lwf, af)] + [ka[:, ln], mf_ref[g]]
                 for g, ln in enumerate(lanes)]
        ins_b = [[ref[sb, ln] for ref in (rb, kb, vb, kkb, lwb, ab)] + [ka[:, ln], mb_ref[g]]
                 for g, ln in enumerate(lanes)]
        outs = _interleave([_scan_chunk(True, *args, cm) for args in ins_f]
                           + [_scan_chunk(False, *args, cm) for args in ins_b])
        for g, ln in enumerate(lanes):
            yf_ref[sf, ln], mf_ref[g] = outs[g]
            yb_ref[sb, ln], mb_ref[g] = outs[SCAN_GROUPS + g]
        return carry

    lax.fori_loop(0, n_chunks, body, 0)


def _wkv_scan(r, k, v, kk, lw_f, lw_b, a_f, a_b, k_a, batch, seq):
    t, d = r.shape
    ns = seq // SCAN_BLOCK
    lanes = SCAN_GROUPS * SCAN_W
    fwd = pl.BlockSpec((SCAN_BLOCK, lanes), lambda b, g, s: (b * ns + s, g))
    bwd = pl.BlockSpec((SCAN_BLOCK, lanes), lambda b, g, s: (b * ns + ns - 1 - s, g))
    state = pltpu.VMEM((SCAN_GROUPS, SCAN_W, SCAN_W), F32)
    return pl.pallas_call(
        functools.partial(_scan_kernel, n_chunks=SCAN_BLOCK // SCAN_CHUNK),
        out_shape=(jax.ShapeDtypeStruct((t, d), F32), jax.ShapeDtypeStruct((t, d), F32)),
        grid=(batch, d // lanes, ns),
        in_specs=[fwd] * 6 + [bwd] * 6 + [pl.BlockSpec((1, lanes), lambda b, g, s: (0, g))],
        out_specs=(fwd, bwd),
        scratch_shapes=[state, state],
        compiler_params=pltpu.CompilerParams(dimension_semantics=("parallel", "parallel", "arbitrary"),
                                             vmem_limit_bytes=VMEM_LIMIT),
        name="wkv_scan",
    )(r, k, v, kk, lw_f, a_f, r, k, v, kk, lw_b, a_b, k_a.reshape(1, d))


def _mix_post_kernel(yf_ref, yb_ref, g_ref, bonus_ref, pc_ref, pcp_ref, pcn_ref, pg_ref, x_ref, mod_ref,
                     lnxw_ref, lnxb_ref, convw_ref, convb_ref, cnw_ref, cnb_ref, cpw_ref, cpb_ref,
                     gateb_ref, wo_ref, ln1w_ref, ln1b_ref, seg_ref, segt_ref,
                     x1_ref, u2hi_ref, u2lo_ref, hext_ref, *, blocks_per_seq, tb, alpha):
    d = D_MODEL
    i = pl.program_id(0)
    keep_prev = jnp.where(i % blocks_per_seq == 0, 0.0, 1.0)
    keep_next = jnp.where(i % blocks_per_seq == blocks_per_seq - 1, 0.0, 1.0)
    glu = lambda p: p[:, :d] * _sigmoid(p[:, d:])

    hext_ref[0:CONV_HALO, :] = glu(pcp_ref[...]) * keep_prev
    hext_ref[CONV_HALO:CONV_HALO + tb, :] = glu(pc_ref[...])
    hext_ref[CONV_HALO + tb:2 * CONV_HALO + tb, :] = glu(pcn_ref[...]) * keep_next
    acc = jnp.zeros((tb, d), F32) + convb_ref[...]
    first_off = CONV_HALO - CONV_PAD
    for res in range(8):
        part = None
        for j in range(CONV_WIDTH):
            if (j + first_off) % 8 != res:
                continue
            term = convw_ref[j:j + 1, :] * hext_ref[pl.ds(j + first_off - res, tb + 8), :]
            part = term if part is None else part + term
        acc = acc + part[res:res + tb]
    hc = _silu(_normalize(acc) * cnw_ref[...] + cnb_ref[...])
    y_conv = _mm(hc, cpw_ref[...]) + cpb_ref[...]

    seg, segt = seg_ref[...], segt_ref[...]
    y = yf_ref[...] + yb_ref[...]
    mu = _head_sum(y, seg, segt) * (1.0 / HEAD)
    yc = y - mu
    var = _head_sum(yc * yc, seg, segt) * (1.0 / HEAD)
    y_rwkv = (yc * lax.rsqrt(var + GN_EPS) * lnxw_ref[...] + lnxb_ref[...] + bonus_ref[...]) * g_ref[...]

    gates = _sigmoid(pg_ref[...] + gateb_ref[...])
    mix = _mm(gates[:, :d] * y_rwkv + gates[:, d:] * y_conv, wo_ref[...])
    m = mod_ref[0]
    x1 = _normalize(alpha * x_ref[...] + (1.0 + m[2:3]) * mix) * ln1w_ref[...] + ln1b_ref[...]
    x1_ref[...] = x1
    u2 = _normalize(x1) * (1.0 + m[4:5]) + m[3:4]
    u2hi_ref[...], u2lo_ref[...] = _split_bf16(u2)


def _mix_post(y_f, y_b, g, bonus, p_conv, p_gate, x2, mod, seq, alpha, lnx_w, lnx_b, conv_w, conv_b,
              cnorm_w, cnorm_b, cpw_w, cpw_b, gate_b, w_o, ln1_w, ln1_b):
    t, d = x2.shape
    tb = 256
    hb = CONV_HALO
    seg, segt = _head_indicator()
    row = lambda x: x.reshape(1, -1)
    const = lambda shape: pl.BlockSpec(shape, lambda i: (0,) * len(shape))
    tok = pl.BlockSpec((tb, d), lambda i: (i, 0))
    tok2 = pl.BlockSpec((tb, 2 * d), lambda i: (i, 0))
    return pl.pallas_call(
        functools.partial(_mix_post_kernel, blocks_per_seq=seq // tb, tb=tb, alpha=alpha),
        out_shape=(jax.ShapeDtypeStruct((t, d), F32), jax.ShapeDtypeStruct((t, d), BF16),
                   jax.ShapeDtypeStruct((t, d), BF16)),
        grid=(t // tb,),
        in_specs=[tok, tok, tok, tok, tok2,
                  pl.BlockSpec((hb, 2 * d), lambda i: (jnp.maximum(i * (tb // hb) - 1, 0), 0)),
                  pl.BlockSpec((hb, 2 * d), lambda i: (jnp.minimum((i + 1) * (tb // hb), t // hb - 1), 0)),
                  tok2, tok,
                  pl.BlockSpec((1, 6, d), lambda i: (i // (seq // tb), 0, 0)),
                  const((1, d)), const((1, d)), const((CONV_WIDTH, d)), const((1, d)), const((1, d)),
                  const((1, d)), const((d, d)), const((1, d)), const((1, 2 * d)), const((d, d)),
                  const((1, d)), const((1, d)), const((d, 128)), const((128, d))],
        out_specs=(tok, tok, tok),
        scratch_shapes=[pltpu.VMEM((tb + 2 * CONV_HALO, d), F32)],
        compiler_params=pltpu.CompilerParams(dimension_semantics=("parallel",),
                                             vmem_limit_bytes=VMEM_LIMIT),
        name="mix_post",
    )(y_f, y_b, g, bonus, p_conv, p_conv, p_conv, p_gate, x2, mod,
      row(lnx_w), row(lnx_b), conv_w, row(conv_b), row(cnorm_w), row(cnorm_b), cpw_w.astype(BF16),
      row(cpw_b), row(gate_b), w_o.astype(BF16), row(ln1_w), row(ln1_b), seg, segt)


def _router_kernel(uhi_ref, ulo_ref, rwhi_ref, rwlo_ref, rb_ref, gate_ref, *, tb):
    neg = -jnp.inf
    nt = lambda a, b: lax.dot_general(a, b, (((1,), (1,)), ((), ())), preferred_element_type=F32)
    u_hi = uhi_ref[...]
    logits = nt(rwhi_ref[...], u_hi) + (nt(rwhi_ref[...], ulo_ref[...]) + nt(rwlo_ref[...], u_hi))
    scores = _sigmoid(logits)
    sel = scores + rb_ref[...]
    sub = lax.broadcasted_iota(jnp.int32, (GROUP_SIZE, tb), 0)
    groups = [sel[GROUP_SIZE * g:GROUP_SIZE * (g + 1), :] for g in range(N_GROUPS)]

    gscore = []
    for xg in groups:
        m1 = jnp.max(xg, axis=0, keepdims=True)
        first = jnp.min(jnp.where(xg == m1, sub, GROUP_SIZE), axis=0, keepdims=True)
        m2 = jnp.max(jnp.where(sub == first, neg, xg), axis=0, keepdims=True)
        gscore.append(m1 + m2)
    cur = []
    for g, xg in enumerate(groups):
        rank = jnp.zeros((1, tb), jnp.int32)
        for g2 in range(N_GROUPS):
            if g2 == g:
                continue
            ahead = gscore[g2] > gscore[g]
            if g2 < g:
                ahead = ahead | (gscore[g2] == gscore[g])
            rank = rank + ahead.astype(jnp.int32)
        cur.append(jnp.where(rank < TOPK_GROUPS, xg, neg))
    chosen = [jnp.zeros((GROUP_SIZE, tb), jnp.bool_) for _ in range(N_GROUPS)]
    for _ in range(TOP_K):
        best = cur[0]
        for xg in cur[1:]:
            best = jnp.maximum(best, xg)
        best = jnp.max(best, axis=0, keepdims=True)
        cand = jnp.full((GROUP_SIZE, tb), N_EXPERTS, jnp.int32)
        for g, xg in enumerate(cur):
            cand = jnp.minimum(cand, jnp.where(xg == best, sub + GROUP_SIZE * g, N_EXPERTS))
        pick = jnp.min(cand, axis=0, keepdims=True)
        for g in range(N_GROUPS):
            hit = (sub + GROUP_SIZE * g) == pick
            chosen[g] = chosen[g] | hit
            cur[g] = jnp.where(hit, neg, cur[g])
    picked = jnp.concatenate(
        [jnp.where(chosen[g], scores[GROUP_SIZE * g:GROUP_SIZE * (g + 1), :], 0.0) for g in range(N_GROUPS)], 0)
    gate_t = picked / jnp.sum(picked, axis=0, keepdims=True) * ROUTED_SCALE

    pc = ROUTER_PIECE
    ident = (lax.broadcasted_iota(jnp.int32, (pc, pc), 0)
             == lax.broadcasted_iota(jnp.int32, (pc, pc), 1)).astype(BF16)
    for s in range(tb // pc):
        rest = gate_t[:, s * pc:(s + 1) * pc]
        out = jnp.zeros((pc, N_EXPERTS), F32)
        for _ in range(3):
            part = rest.astype(BF16)
            rest = rest - part.astype(F32)
            out = out + lax.dot_general(ident, part, (((1,), (1,)), ((), ())), preferred_element_type=F32)
        gate_ref[s * pc:(s + 1) * pc, :] = out


def _router(u2_hi, u2_lo, router_w, router_b):
    t, d = u2_hi.shape
    tb = ROUTER_TB
    rw_hi, rw_lo = _split_bf16(router_w.T)
    return pl.pallas_call(
        functools.partial(_router_kernel, tb=tb),
        out_shape=jax.ShapeDtypeStruct((t, N_EXPERTS), F32),
        grid=(t // tb,),
        in_specs=[pl.BlockSpec((tb, d), lambda i: (i, 0)),
                  pl.BlockSpec((tb, d), lambda i: (i, 0)),
                  pl.BlockSpec((N_EXPERTS, d), lambda i: (0, 0)),
                  pl.BlockSpec((N_EXPERTS, d), lambda i: (0, 0)),
                  pl.BlockSpec((N_EXPERTS, 1), lambda i: (0, 0))],
        out_specs=pl.BlockSpec((tb, N_EXPERTS), lambda i: (i, 0)),
        compiler_params=pltpu.CompilerParams(dimension_semantics=("parallel",),
                                             vmem_limit_bytes=VMEM_LIMIT),
        name="router",
    )(u2_hi, u2_lo, rw_hi, rw_lo, router_b.reshape(N_EXPERTS, 1))


def _moe_kernel(u_ref, gate_ref, x1_ref, mod_ref, wg_ref, wu_ref, wd_ref, shg_ref, shu_ref, shd_ref,
                ln2w_ref, ln2b_ref, out_ref, *, alpha, n_steps):
    e = pl.program_id(1)
    ub = u_ref[...]

    @pl.when(e == 0)
    def _():
        hs = _silu(jnp.dot(ub, shg_ref[...], preferred_element_type=F32)) * jnp.dot(
            ub, shu_ref[...], preferred_element_type=F32)
        out_ref[...] = _mm(hs, shd_ref[...])

    pick = (lax.broadcasted_iota(jnp.int32, (N_EXPERTS, 128), 0)
            == lax.broadcasted_iota(jnp.int32, (N_EXPERTS, 128), 1) + e * MOE_EXPERTS_PER_STEP)
    gate = _mm_split(gate_ref[...], pick.astype(BF16))
    acc = out_ref[...]
    for j in range(MOE_EXPERTS_PER_STEP):
        h = _silu(jnp.dot(ub, wg_ref[j], preferred_element_type=F32)) * jnp.dot(
            ub, wu_ref[j], preferred_element_type=F32)
        acc = acc + _mm(h * gate[:, j:j + 1], wd_ref[j])
    out_ref[...] = acc

    @pl.when(e == n_steps - 1)
    def _():
        m = mod_ref[0]
        out_ref[...] = (_normalize(alpha * x1_ref[...] + (1.0 + m[5:6]) * out_ref[...]) * ln2w_ref[...]
                        + ln2b_ref[...])


def _moe(u2, gate, x1, mod, seq, alpha, exp_gate, exp_up, exp_down, sh_gate, sh_up, sh_down, ln2_w, ln2_b):
    t, d = u2.shape
    tb = 1024
    eps = MOE_EXPERTS_PER_STEP
    n_steps = N_EXPERTS // eps
    row = lambda x: x.reshape(1, -1)
    const = lambda shape: pl.BlockSpec(shape, lambda i, e: (0,) * len(shape))
    resident = lambda shape: pl.BlockSpec(shape, lambda i, e: (0,) * len(shape), pipeline_mode=pl.Buffered(1))
    tok = pl.BlockSpec((tb, d), lambda i, e: (i, 0))
    return pl.pallas_call(
        functools.partial(_moe_kernel, alpha=alpha, n_steps=n_steps),
        out_shape=jax.ShapeDtypeStruct((t, d), F32),
        grid=(t // tb, n_steps),
        in_specs=[tok,
                  pl.BlockSpec((tb, N_EXPERTS), lambda i, e: (i, 0)),
                  tok,
                  pl.BlockSpec((1, 6, d), lambda i, e: (i // (seq // tb), 0, 0)),
                  pl.BlockSpec((eps, d, D_EXPERT), lambda i, e: (e, 0, 0)),
                  pl.BlockSpec((eps, d, D_EXPERT), lambda i, e: (e, 0, 0)),
                  pl.BlockSpec((eps, D_EXPERT, d), lambda i, e: (e, 0, 0)),
                  resident(sh_gate.shape), resident(sh_up.shape), resident(sh_down.shape),
                  const((1, d)), const((1, d))],
        out_specs=tok,
        compiler_params=pltpu.CompilerParams(dimension_semantics=("parallel", "arbitrary"),
                                             vmem_limit_bytes=MOE_VMEM_LIMIT),
        name="moe",
    )(u2, gate, x1, mod, exp_gate.astype(BF16), exp_up.astype(BF16), exp_down.astype(BF16),
      sh_gate.astype(BF16), sh_up.astype(BF16), sh_down.astype(BF16), row(ln2_w), row(ln2_b))


def kernel(x, c, ada_w, ada_b, w_in, shift_mu, w0, w_up, a0, a_up, g_up, k_k, k_a, r_k, lnx_w, lnx_b,
           conv_w, conv_b, cnorm_w, cnorm_b, cpw_w, cpw_b, gate_b, w_o, ln1_w, ln1_b, router_w, router_b,
           exp_gate, exp_up, exp_down, sh_gate, sh_up, sh_down, ln2_w, ln2_b):
    batch, seq, d = x.shape
    depth = ada_w.shape[0]
    alpha = (2.0 * depth) ** 0.25
    x2 = x.reshape(batch * seq, d)
    for l in range(depth):
        mod = _adaln(c, ada_w[l], ada_b[l])
        p_rw, p_conv, p_gate = _in_proj(x2, mod, w_in[l], seq)
        r, k, v, kk, lw_f, lw_b, a_f, a_b, g, bonus = _rwkv_prep(
            p_rw, seq, shift_mu[l], w0[l], w_up[l], a0[l], a_up[l], g_up[l], k_k[l], k_a[l], r_k[l])
        y_f, y_b = _wkv_scan(r, k, v, kk, lw_f, lw_b, a_f, a_b, k_a[l], batch, seq)
        x1, u2_hi, u2_lo = _mix_post(y_f, y_b, g, bonus, p_conv, p_gate, x2, mod, seq, alpha, lnx_w[l],
                                     lnx_b[l], conv_w[l], conv_b[l], cnorm_w[l], cnorm_b[l], cpw_w[l],
                                     cpw_b[l], gate_b[l], w_o[l], ln1_w[l], ln1_b[l])
        gate = _router(u2_hi, u2_lo, router_w[l], router_b[l])
        x2 = _moe(u2_hi, gate, x1, mod, seq, alpha, exp_gate[l], exp_up[l], exp_down[l], sh_gate[l],
                  sh_up[l], sh_down[l], ln2_w[l], ln2_b[l])
    return x2.reshape(batch, seq, d)
```

```python
import functools
import math

import jax
import jax.numpy as jnp
from jax import lax
from jax.experimental import pallas as pl
from jax.experimental.pallas import tpu as pltpu

F32 = jnp.float32
BF16 = jnp.bfloat16
HIGHEST = lax.Precision.HIGHEST

D_MODEL = 1024
HEAD = 64
HEADS = D_MODEL // HEAD
LORA_G = 128
RW_COLS = 3 * D_MODEL + 4 * 64 + LORA_G
DECAY_SCALE = math.exp(-0.5)
GN_EPS = 64e-5
LN_EPS = 1e-5
CONV_WIDTH = 31
CONV_PAD = CONV_WIDTH // 2
CONV_HALO = 16
N_EXPERTS = 64
TOP_K = 8
N_GROUPS = 8
GROUP_SIZE = N_EXPERTS // N_GROUPS
TOPK_GROUPS = 4
D_EXPERT = 256
ROUTED_SCALE = 2.5

SCAN_HG = 2
SCAN_W = SCAN_HG * HEAD
SCAN_CHUNK = 64
SCAN_R = SCAN_HG * SCAN_CHUNK
SCAN_GROUPS = 8
SCAN_BLOCK = 256

ROUTER_TB = 4096
ROUTER_PIECE = 512
MOE_EXPERTS_PER_STEP = 8
VMEM_LIMIT = 56 * 1024 * 1024
MOE_VMEM_LIMIT = 60 * 1024 * 1024


def _mm(a, b):
    return jnp.dot(a.astype(BF16), b.astype(BF16), preferred_element_type=F32)


def _mm_nt(a, b):
    return lax.dot_general(a.astype(BF16), b.astype(BF16), (((1,), (1,)), ((), ())),
                           preferred_element_type=F32)


def _mm_tn(a, b):
    return lax.dot_general(a.astype(BF16), b.astype(BF16), (((0,), (0,)), ((), ())),
                           preferred_element_type=F32)


def _mm_f32(a, b):
    return jnp.dot(a, b, precision=HIGHEST, preferred_element_type=F32)


def _normalize(x):
    mu = jnp.mean(x, axis=-1, keepdims=True)
    xc = x - mu
    var = jnp.mean(xc * xc, axis=-1, keepdims=True)
    return xc * lax.rsqrt(var + LN_EPS)


def _sigmoid(x):
    return jax.nn.sigmoid(x)


def _silu(x):
    return x * jax.nn.sigmoid(x)


def _split_bf16(x):
    hi = x.astype(BF16)
    return hi, (x - hi.astype(F32)).astype(BF16)


def _mm_split(x, w01):
    hi, lo = _split_bf16(x)
    return (jnp.dot(hi, w01, preferred_element_type=F32) + jnp.dot(lo, w01, preferred_element_type=F32))


def _head_sum(x, seg, segt):
    return _mm_split(_mm_split(x, seg), segt)


def _adaln_kernel(c_ref, w_ref, b_ref, o_ref):
    o_ref[...] = _mm_f32(_silu(c_ref[...]), w_ref[...]) + b_ref[...]


def _adaln(c, ada_w, ada_b):
    b, d = c.shape
    n = ada_w.shape[1]
    rows = 8
    cp = jnp.zeros((rows, d), F32).at[:b].set(c)
    tn = 1536
    out = pl.pallas_call(
        _adaln_kernel,
        out_shape=jax.ShapeDtypeStruct((rows, n), F32),
        grid=(n // tn,),
        in_specs=[pl.BlockSpec((rows, d), lambda j: (0, 0)),
                  pl.BlockSpec((d, tn), lambda j: (0, j)),
                  pl.BlockSpec((1, tn), lambda j: (0, j))],
        out_specs=pl.BlockSpec((rows, tn), lambda j: (0, j)),
        name="adaln",
    )(cp, ada_w, ada_b.reshape(1, n))
    return out[:b].reshape(b, 6, d)


def _in_proj_kernel(x_ref, mod_ref, w_ref, prw_ref, pcv_ref, pgt_ref):
    d = D_MODEL
    m = mod_ref[0]
    u = (_normalize(x_ref[...]) * (1.0 + m[1:2]) + m[0:1]).astype(BF16)
    prw_ref[...] = jnp.dot(u, w_ref[:, :RW_COLS], preferred_element_type=F32)
    pcv_ref[...] = jnp.dot(u, w_ref[:, RW_COLS:RW_COLS + 2 * d], preferred_element_type=F32)
    pgt_ref[...] = jnp.dot(u, w_ref[:, RW_COLS + 2 * d:], preferred_element_type=F32)


def _in_proj(x2, mod, w_in, seq):
    t, d = x2.shape
    tm = 512
    w = w_in.astype(BF16)
    resident = lambda shape: pl.BlockSpec(shape, lambda i: (0, 0), pipeline_mode=pl.Buffered(1))
    return pl.pallas_call(
        _in_proj_kernel,
        out_shape=(jax.ShapeDtypeStruct((t, RW_COLS), F32),
                   jax.ShapeDtypeStruct((t, 2 * d), F32),
                   jax.ShapeDtypeStruct((t, 2 * d), F32)),
        grid=(t // tm,),
        in_specs=[pl.BlockSpec((tm, d), lambda i: (i, 0)),
                  pl.BlockSpec((1, 6, d), lambda i: (i // (seq // tm), 0, 0)),
                  resident(w.shape)],
        out_specs=(pl.BlockSpec((tm, RW_COLS), lambda i: (i, 0)),
                   pl.BlockSpec((tm, 2 * d), lambda i: (i, 0)),
                   pl.BlockSpec((tm, 2 * d), lambda i: (i, 0))),
        compiler_params=pltpu.CompilerParams(dimension_semantics=("parallel",),
                                             vmem_limit_bytes=VMEM_LIMIT),
        name="in_proj",
    )(x2, mod, w)


def _rwkv_prep_kernel(p_ref, pp_ref, pn_ref, mu_ref, w0_ref, wup_ref, a0_ref, aup_ref, gup_ref,
                      kk_ref, ka_ref, rk_ref, seg_ref, segt_ref,
                      r_out, k_out, v_out, kk_out, lwf_out, lwb_out, af_out, ab_out, g_out, bonus_out,
                      *, blocks_per_seq, tb):
    d = D_MODEL
    i = pl.program_id(0)
    keep_prev = jnp.where(i % blocks_per_seq == 0, 0.0, 1.0)
    keep_next = jnp.where(i % blocks_per_seq == blocks_per_seq - 1, 0.0, 1.0)
    p = p_ref[...]
    prev_row = pp_ref[7:8, :] * keep_prev
    next_row = pn_ref[0:1, :] * keep_next
    sub = lax.broadcasted_iota(jnp.int32, (8, 1), 0)
    p_prev = pltpu.roll(p, 1, 0)
    p_prev = jnp.concatenate([jnp.where(sub == 0, prev_row, p_prev[:8]), p_prev[8:]], 0)
    p_next = pltpu.roll(p, tb - 1, 0)
    p_next = jnp.concatenate([p_next[:tb - 8], jnp.where(sub == 7, next_row, p_next[tb - 8:])], 0)
    mu_prev, mu_next = mu_ref[0:1, :], mu_ref[1:2, :]
    ps = (1.0 - mu_prev - mu_next) * p + mu_prev * p_prev + mu_next * p_next

    r, k, v = ps[:, :d], ps[:, d:2 * d], ps[:, 2 * d:3 * d]
    wl = ps[:, 3 * d:3 * d + 128]
    al = ps[:, 3 * d + 128:3 * d + 256]
    gl = ps[:, 3 * d + 256:3 * d + 384]
    seg, segt = seg_ref[...], segt_ref[...]

    lw = -DECAY_SCALE * _sigmoid(w0_ref[...] + _mm(jnp.tanh(wl), wup_ref[...]))
    a = _sigmoid(a0_ref[...] + _mm(al, aup_ref[...]))
    g = _mm(_sigmoid(gl), gup_ref[...])
    kraw = k * kk_ref[...]
    knorm = jnp.sqrt(_head_sum(kraw * kraw, seg, segt))
    kk = kraw / jnp.maximum(knorm, 1e-12)
    a_f = a[:, :d]
    k_f = k * (1.0 + (a_f - 1.0) * ka_ref[...])
    bonus = _head_sum(r * k_f * rk_ref[...], seg, segt) * v

    r_out[...] = r
    k_out[...] = k
    v_out[...] = v
    kk_out[...] = kk
    lwf_out[...] = lw[:, :d]
    lwb_out[...] = lw[:, d:]
    af_out[...] = a_f
    ab_out[...] = a[:, d:]
    g_out[...] = g
    bonus_out[...] = bonus


def _block_diag2(a, b):
    za = jnp.zeros((a.shape[0], b.shape[1]), a.dtype)
    zb = jnp.zeros((b.shape[0], a.shape[1]), a.dtype)
    return jnp.concatenate([jnp.concatenate([a, za], 1), jnp.concatenate([zb, b], 1)], 0)


def _head_indicator():
    ch = jnp.arange(D_MODEL) // HEAD
    seg = (ch[:, None] == jnp.arange(128)[None, :]).astype(BF16)
    return seg, seg.T


def _rwkv_prep(p_rw, seq, shift_mu, w0, w_up, a0, a_up, g_up, k_k, k_a, r_k):
    t = p_rw.shape[0]
    d = D_MODEL
    tb = 256
    hb = 8
    seg, segt = _head_indicator()
    wup = _block_diag2(w_up[0], w_up[1]).astype(BF16)
    aup = _block_diag2(a_up[0], a_up[1]).astype(BF16)
    row = lambda x: x.reshape(1, -1)
    const = lambda shape: pl.BlockSpec(shape, lambda i: (0,) * len(shape))
    tok = pl.BlockSpec((tb, d), lambda i: (i, 0))
    outs = tuple(jax.ShapeDtypeStruct((t, d), F32) for _ in range(10))
    return pl.pallas_call(
        functools.partial(_rwkv_prep_kernel, blocks_per_seq=seq // tb, tb=tb),
        out_shape=outs,
        grid=(t // tb,),
        in_specs=[pl.BlockSpec((tb, RW_COLS), lambda i: (i, 0)),
                  pl.BlockSpec((hb, RW_COLS), lambda i: (jnp.maximum(i * (tb // hb) - 1, 0), 0)),
                  pl.BlockSpec((hb, RW_COLS), lambda i: (jnp.minimum((i + 1) * (tb // hb), t // hb - 1), 0)),
                  const((2, RW_COLS)), const((1, 2 * d)), const((128, 2 * d)), const((1, 2 * d)),
                  const((128, 2 * d)), const((LORA_G, d)), const((1, d)), const((1, d)), const((1, d)),
                  const((d, 128)), const((128, d))],
        out_specs=tuple(tok for _ in range(10)),
        compiler_params=pltpu.CompilerParams(dimension_semantics=("parallel",),
                                             vmem_limit_bytes=VMEM_LIMIT),
        name="rwkv_prep",
    )(p_rw, p_rw, p_rw, shift_mu, row(w0), wup, row(a0), aup, g_up.astype(BF16), row(k_k), row(k_a),
      row(r_k), seg, segt)


def _scan_chunk(forward, r, k, v, kk, lw, a, ka, m, cm):
    c_steps, r_rows, w = SCAN_CHUNK, SCAN_R, SCAN_W
    strict = cm["strict_f"] if forward else cm["strict_b"]
    read_mask = cm["incl_f"] if forward else cm["strict_b"]
    same = cm["same"]
    c = lw
    step = 1
    while step < c_steps:
        if forward:
            c = c + jnp.where(cm["t_idx"] >= step, pltpu.roll(c, step, 0), 0.0)
        else:
            c = c + jnp.where(cm["t_idx"] < c_steps - step, pltpu.roll(c, c_steps - step, 0), 0.0)
        step *= 2
    ctot = c[c_steps - 1:c_steps, :] if forward else c[0:1, :]
    p_incl = jnp.exp(c)
    p_excl = jnp.exp(c - lw)
    p_inv = jnp.exp(-c)
    p_tail = jnp.exp(ctot - c)
    b = a * kk
    kd = k * (1.0 + (a - 1.0) * ka)
    kq = kk * p_excl
    rq = r * (p_incl if forward else p_excl)

    tile = lambda x: jnp.concatenate([x] * SCAN_HG, axis=0)
    expand = lambda x: jnp.where(same, tile(x), 0.0)
    fold = lambda x: sum(x[h * c_steps:(h + 1) * c_steps] for h in range(1, SCAN_HG)) + x[0:c_steps]

    kq_e, rq_e, v_e = expand(kq), expand(rq), expand(v)
    sc = _mm_nt(jnp.concatenate([kq_e, rq_e], 0),
                jnp.concatenate([tile(b * p_inv), tile(kd * p_inv)], 0))
    yield
    low = jnp.where(strict, sc[:r_rows, :r_rows], 0.0)
    akk = jnp.where(strict, sc[:r_rows, r_rows:], 0.0)
    arb = jnp.where(read_mask, sc[r_rows:, :r_rows], 0.0)
    ark = jnp.where(read_mask, sc[r_rows:, r_rows:], 0.0)

    eye = cm["eye"]
    ld = jnp.where(cm["m8"], low, 0.0)
    t_inv = eye - ld
    l2 = _mm(ld, ld)
    on_v = _mm(jnp.concatenate([akk.astype(BF16), ark.astype(BF16), expand(kd * p_tail).astype(BF16).T], 0), v_e)
    akk_v, ark_v, kkp_v = on_v[:r_rows], on_v[r_rows:2 * r_rows], on_v[2 * r_rows:]
    yield
    on_l2 = _mm(jnp.concatenate([t_inv, l2], 0), l2)
    t_inv = t_inv + on_l2[:r_rows]
    l4 = on_l2[r_rows:]
    yield
    t_inv = t_inv + _mm(t_inv, l4)
    yield
    for off in cm["offs"]:
        t_lo = _mm(t_inv, jnp.where(off, low, 0.0))
        yield
        t_inv = t_inv - _mm(t_lo, t_inv)
        yield

    z = _mm(t_inv, jnp.concatenate([kq_e, akk_v], 1))
    yield
    kqp_ut = jnp.concatenate([z[:, :w], -z[:, w:]], 1)
    on_ku = _mm(jnp.concatenate([arb.astype(BF16), expand(b * p_tail).astype(BF16).T], 0), kqp_ut)
    q2, g1 = on_ku[:r_rows], on_ku[r_rows:]
    yield
    rqp = fold(rq_e - q2[:, :w])
    y0 = fold(q2[:, w:] + ark_v)
    a_mat = jnp.where(eye > 0.5, jnp.exp(ctot), 0.0) - g1[:, :w]
    b_mat = g1[:, w:] + kkp_v
    on_m = _mm(jnp.concatenate([rqp, a_mat], 0), m)
    return on_m[:c_steps] + y0, on_m[c_steps:] + b_mat


def _interleave(gens):
    results = [None] * len(gens)
    active = list(range(len(gens)))
    while active:
        for i in list(active):
            try:
                next(gens[i])
            except StopIteration as done:
                results[i] = done.value
                active.remove(i)
    return results


def _scan_kernel(rf, kf, vf, kkf, lwf, af, rb, kb, vb, kkb, lwb, ab, ka_ref, yf_ref, yb_ref,
                 mf_ref, mb_ref, *, n_chunks):
    @pl.when(pl.program_id(2) == 0)
    def _():
        mf_ref[...] = jnp.zeros_like(mf_ref)
        mb_ref[...] = jnp.zeros_like(mb_ref)

    c_steps, r_rows = SCAN_CHUNK, SCAN_R
    row = lax.broadcasted_iota(jnp.int32, (r_rows, r_rows), 0)
    col = lax.broadcasted_iota(jnp.int32, (r_rows, r_rows), 1)
    same = (row // c_steps) == (col // c_steps)
    cm = {
        "same": same,
        "strict_f": same & (col < row), "incl_f": same & (col <= row), "strict_b": same & (col > row),
        "m8": (row // 8) == (col // 8),
        "offs": [((row // (2 * s)) == (col // (2 * s))) & ((row // s) != (col // s)) for s in (8, 16, 32)],
        "eye": (row == col).astype(F32),
        "t_idx": lax.broadcasted_iota(jnp.int32, (c_steps, SCAN_W), 0),
    }
    ka = ka_ref[...]

    def body(j, carry):
        of = pl.multiple_of(j * c_steps, c_steps)
        ob = pl.multiple_of((n_chunks - 1 - j) * c_steps, c_steps)
        sf, sb = pl.ds(of, c_steps), pl.ds(ob, c_steps)
        lanes = [slice(g * SCAN_W, (g + 1) * SCAN_W) for g in range(SCAN_GROUPS)]
        ins_f = [[ref[sf, ln] for ref in (rf, kf, vf, kkf, lwf, af)] + [ka[:, ln], mf_ref[g]]
                 for g, ln in enumerate(lanes)]
        ins_b = [[ref[sb, ln] for ref in (rb, kb, vb, kkb, lwb, ab)] + [ka[:, ln], mb_ref[g]]
                 for g, ln in enumerate(lanes)]
        outs = _interleave([_scan_chunk(True, *args, cm) for args in ins_f]
                           + [_scan_chunk(False, *args, cm) for args in ins_b])
        for g, ln in enumerate(lanes):
            yf_ref[sf, ln], mf_ref[g] = outs[g]
            yb_ref[sb, ln], mb_ref[g] = outs[SCAN_GROUPS + g]
        return carry

    lax.fori_loop(0, n_chunks, body, 0)


def _wkv_scan(r, k, v, kk, lw_f, lw_b, a_f, a_b, k_a, batch, seq):
    t, d = r.shape
    ns = seq // SCAN_BLOCK
    lanes = SCAN_GROUPS * SCAN_W
    fwd = pl.BlockSpec((SCAN_BLOCK, lanes), lambda b, g, s: (b * ns + s, g))
    bwd = pl.BlockSpec((SCAN_BLOCK, lanes), lambda b, g, s: (b * ns + ns - 1 - s, g))
    state = pltpu.VMEM((SCAN_GROUPS, SCAN_W, SCAN_W), F32)
    return pl.pallas_call(
        functools.partial(_scan_kernel, n_chunks=SCAN_BLOCK // SCAN_CHUNK),
        out_shape=(jax.ShapeDtypeStruct((t, d), F32), jax.ShapeDtypeStruct((t, d), F32)),
        grid=(batch, d // lanes, ns),
        in_specs=[fwd] * 6 + [bwd] * 6 + [pl.BlockSpec((1, lanes), lambda b, g, s: (0, g))],
        out_specs=(fwd, bwd),
        scratch_shapes=[state, state],
        compiler_params=pltpu.CompilerParams(dimension_semantics=("parallel", "parallel", "arbitrary"),
                                             vmem_limit_bytes=VMEM_LIMIT),
        name="wkv_scan",
    )(r, k, v, kk, lw_f, a_f, r, k, v, kk, lw_b, a_b, k_a.reshape(1, d))


def _mix_post_kernel(yf_ref, yb_ref, g_ref, bonus_ref, pc_ref, pcp_ref, pcn_ref, pg_ref, x_ref, mod_ref,
                     lnxw_ref, lnxb_ref, convw_ref, convb_ref, cnw_ref, cnb_ref, cpw_ref, cpb_ref,
                     gateb_ref, wo_ref, ln1w_ref, ln1b_ref, seg_ref, segt_ref,
                     x1_ref, u2hi_ref, u2lo_ref, hext_ref, *, blocks_per_seq, tb, alpha):
    d = D_MODEL
    i = pl.program_id(0)
    keep_prev = jnp.where(i % blocks_per_seq == 0, 0.0, 1.0)
    keep_next = jnp.where(i % blocks_per_seq == blocks_per_seq - 1, 0.0, 1.0)
    glu = lambda p: p[:, :d] * _sigmoid(p[:, d:])

    hext_ref[0:CONV_HALO, :] = glu(pcp_ref[...]) * keep_prev
    hext_ref[CONV_HALO:CONV_HALO + tb, :] = glu(pc_ref[...])
    hext_ref[CONV_HALO + tb:2 * CONV_HALO + tb, :] = glu(pcn_ref[...]) * keep_next
    acc = jnp.zeros((tb, d), F32) + convb_ref[...]
    first_off = CONV_HALO - CONV_PAD
    for res in range(8):
        part = None
        for j in range(CONV_WIDTH):
            if (j + first_off) % 8 != res:
                continue
            term = convw_ref[j:j + 1, :] * hext_ref[pl.ds(j + first_off - res, tb + 8), :]
            part = term if part is None else part + term
        acc = acc + part[res:res + tb]
    hc = _silu(_normalize(acc) * cnw_ref[...] + cnb_ref[...])
    y_conv = _mm(hc, cpw_ref[...]) + cpb_ref[...]

    seg, segt = seg_ref[...], segt_ref[...]
    y = yf_ref[...] + yb_ref[...]
    mu = _head_sum(y, seg, segt) * (1.0 / HEAD)
    yc = y - mu
    var = _head_sum(yc * yc, seg, segt) * (1.0 / HEAD)
    y_rwkv = (yc * lax.rsqrt(var + GN_EPS) * lnxw_ref[...] + lnxb_ref[...] + bonus_ref[...]) * g_ref[...]

    gates = _sigmoid(pg_ref[...] + gateb_ref[...])
    mix = _mm(gates[:, :d] * y_rwkv + gates[:, d:] * y_conv, wo_ref[...])
    m = mod_ref[0]
    x1 = _normalize(alpha * x_ref[...] + (1.0 + m[2:3]) * mix) * ln1w_ref[...] + ln1b_ref[...]
    x1_ref[...] = x1
    u2 = _normalize(x1) * (1.0 + m[4:5]) + m[3:4]
    u2hi_ref[...], u2lo_ref[...] = _split_bf16(u2)


def _mix_post(y_f, y_b, g, bonus, p_conv, p_gate, x2, mod, seq, alpha, lnx_w, lnx_b, conv_w, conv_b,
              cnorm_w, cnorm_b, cpw_w, cpw_b, gate_b, w_o, ln1_w, ln1_b):
    t, d = x2.shape
    tb = 256
    hb = CONV_HALO
    seg, segt = _head_indicator()
    row = lambda x: x.reshape(1, -1)
    const = lambda shape: pl.BlockSpec(shape, lambda i: (0,) * len(shape))
    tok = pl.BlockSpec((tb, d), lambda i: (i, 0))
    tok2 = pl.BlockSpec((tb, 2 * d), lambda i: (i, 0))
    return pl.pallas_call(
        functools.partial(_mix_post_kernel, blocks_per_seq=seq // tb, tb=tb, alpha=alpha),
        out_shape=(jax.ShapeDtypeStruct((t, d), F32), jax.ShapeDtypeStruct((t, d), BF16),
                   jax.ShapeDtypeStruct((t, d), BF16)),
        grid=(t // tb,),
        in_specs=[tok, tok, tok, tok, tok2,
                  pl.BlockSpec((hb, 2 * d), lambda i: (jnp.maximum(i * (tb // hb) - 1, 0), 0)),
                  pl.BlockSpec((hb, 2 * d), lambda i: (jnp.minimum((i + 1) * (tb // hb), t // hb - 1), 0)),
                  tok2, tok,
                  pl.BlockSpec((1, 6, d), lambda i: (i // (seq // tb), 0, 0)),
                  const((1, d)), const((1, d)), const((CONV_WIDTH, d)), const((1, d)), const((1, d)),
                  const((1, d)), const((d, d)), const((1, d)), const((1, 2 * d)), const((d, d)),
                  const((1, d)), const((1, d)), const((d, 128)), const((128, d))],
        out_specs=(tok, tok, tok),
        scratch_shapes=[pltpu.VMEM((tb + 2 * CONV_HALO, d), F32)],
        compiler_params=pltpu.CompilerParams(dimension_semantics=("parallel",),
                                             vmem_limit_bytes=VMEM_LIMIT),
        name="mix_post",
    )(y_f, y_b, g, bonus, p_conv, p_conv, p_conv, p_gate, x2, mod,
      row(lnx_w), row(lnx_b), conv_w, row(conv_b), row(cnorm_w), row(cnorm_b), cpw_w.astype(BF16),
      row(cpw_b), row(gate_b), w_o.astype(BF16), row(ln1_w), row(ln1_b), seg, segt)


def _router_kernel(uhi_ref, ulo_ref, rwhi_ref, rwlo_ref, rb_ref, gate_ref, *, tb):
    neg = -jnp.inf
    nt = lambda a, b: lax.dot_general(a, b, (((1,), (1,)), ((), ())), preferred_element_type=F32)
    u_hi = uhi_ref[...]
    logits = nt(rwhi_ref[...], u_hi) + (nt(rwhi_ref[...], ulo_ref[...]) + nt(rwlo_ref[...], u_hi))
    scores = _sigmoid(logits)
    sel = scores + rb_ref[...]
    sub = lax.broadcasted_iota(jnp.int32, (GROUP_SIZE, tb), 0)
    groups = [sel[GROUP_SIZE * g:GROUP_SIZE * (g + 1), :] for g in range(N_GROUPS)]

    gscore = []
    for xg in groups:
        m1 = jnp.max(xg, axis=0, keepdims=True)
        first = jnp.min(jnp.where(xg == m1, sub, GROUP_SIZE), axis=0, keepdims=True)
        m2 = jnp.max(jnp.where(sub == first, neg, xg), axis=0, keepdims=True)
        gscore.append(m1 + m2)
    cur = []
    for g, xg in enumerate(groups):
        rank = jnp.zeros((1, tb), jnp.int32)
        for g2 in range(N_GROUPS):
            if g2 == g:
                continue
            ahead = gscore[g2] > gscore[g]
            if g2 < g:
                ahead = ahead | (gscore[g2] == gscore[g])
            rank = rank + ahead.astype(jnp.int32)
        cur.append(jnp.where(rank < TOPK_GROUPS, xg, neg))
    chosen = [jnp.zeros((GROUP_SIZE, tb), jnp.bool_) for _ in range(N_GROUPS)]
    for _ in range(TOP_K):
        best = cur[0]
        for xg in cur[1:]:
            best = jnp.maximum(best, xg)
        best = jnp.max(best, axis=0, keepdims=True)
        cand = jnp.full((GROUP_SIZE, tb), N_EXPERTS, jnp.int32)
        for g, xg in enumerate(cur):
            cand = jnp.minimum(cand, jnp.where(xg == best, sub + GROUP_SIZE * g, N_EXPERTS))
        pick = jnp.min(cand, axis=0, keepdims=True)
        for g in range(N_GROUPS):
            hit = (sub + GROUP_SIZE * g) == pick
            chosen[g] = chosen[g] | hit
            cur[g] = jnp.where(hit, neg, cur[g])
    picked = jnp.concatenate(
        [jnp.where(chosen[g], scores[GROUP_SIZE * g:GROUP_SIZE * (g + 1), :], 0.0) for g in range(N_GROUPS)], 0)
    gate_t = picked / jnp.sum(picked, axis=0, keepdims=True) * ROUTED_SCALE

    pc = ROUTER_PIECE
    ident = (lax.broadcasted_iota(jnp.int32, (pc, pc), 0)
             == lax.broadcasted_iota(jnp.int32, (pc, pc), 1)).astype(BF16)
    for s in range(tb // pc):
        rest = gate_t[:, s * pc:(s + 1) * pc]
        out = jnp.zeros((pc, N_EXPERTS), F32)
        for _ in range(3):
            part = rest.astype(BF16)
            rest = rest - part.astype(F32)
            out = out + lax.dot_general(ident, part, (((1,), (1,)), ((), ())), preferred_element_type=F32)
        gate_ref[s * pc:(s + 1) * pc, :] = out


def _router(u2_hi, u2_lo, router_w, router_b):
    t, d = u2_hi.shape
    tb = ROUTER_TB
    rw_hi, rw_lo = _split_bf16(router_w.T)
    return pl.pallas_call(
        functools.partial(_router_kernel, tb=tb),
        out_shape=jax.ShapeDtypeStruct((t, N_EXPERTS), F32),
        grid=(t // tb,),
        in_specs=[pl.BlockSpec((tb, d), lambda i: (i, 0)),
                  pl.BlockSpec((tb, d), lambda i: (i, 0)),
                  pl.BlockSpec((N_EXPERTS, d), lambda i: (0, 0)),
                  pl.BlockSpec((N_EXPERTS, d), lambda i: (0, 0)),
                  pl.BlockSpec((N_EXPERTS, 1), lambda i: (0, 0))],
        out_specs=pl.BlockSpec((tb, N_EXPERTS), lambda i: (i, 0)),
        compiler_params=pltpu.CompilerParams(dimension_semantics=("parallel",),
                                             vmem_limit_bytes=VMEM_LIMIT),
        name="router",
    )(u2_hi, u2_lo, rw_hi, rw_lo, router_b.reshape(N_EXPERTS, 1))


def _moe_kernel(u_ref, gate_ref, x1_ref, mod_ref, wg_ref, wu_ref, wd_ref, shg_ref, shu_ref, shd_ref,
                ln2w_ref, ln2b_ref, out_ref, *, alpha, n_steps):
    e = pl.program_id(1)
    ub = u_ref[...]

    @pl.when(e == 0)
    def _():
        hs = _silu(jnp.dot(ub, shg_ref[...], preferred_element_type=F32)) * jnp.dot(
            ub, shu_ref[...], preferred_element_type=F32)
        out_ref[...] = _mm(hs, shd_ref[...])

    pick = (lax.broadcasted_iota(jnp.int32, (N_EXPERTS, 128), 0)
            == lax.broadcasted_iota(jnp.int32, (N_EXPERTS, 128), 1) + e * MOE_EXPERTS_PER_STEP)
    gate = _mm_split(gate_ref[...], pick.astype(BF16))
    acc = out_ref[...]
    for j in range(MOE_EXPERTS_PER_STEP):
        h = _silu(jnp.dot(ub, wg_ref[j], preferred_element_type=F32)) * jnp.dot(
            ub, wu_ref[j], preferred_element_type=F32)
        acc = acc + _mm(h * gate[:, j:j + 1], wd_ref[j])
    out_ref[...] = acc

    @pl.when(e == n_steps - 1)
    def _():
        m = mod_ref[0]
        out_ref[...] = (_normalize(alpha * x1_ref[...] + (1.0 + m[5:6]) * out_ref[...]) * ln2w_ref[...]
                        + ln2b_ref[...])


def _moe(u2, gate, x1, mod, seq, alpha, exp_gate, exp_up, exp_down, sh_gate, sh_up, sh_down, ln2_w, ln2_b):
    t, d = u2.shape
    tb = 1024
    eps = MOE_EXPERTS_PER_STEP
    n_steps = N_EXPERTS // eps
    row = lambda x: x.reshape(1, -1)
    const = lambda shape: pl.BlockSpec(shape, lambda i, e: (0,) * len(shape))
    resident = lambda shape: pl.BlockSpec(shape, lambda i, e: (0,) * len(shape), pipeline_mode=pl.Buffered(1))
    tok = pl.BlockSpec((tb, d), lambda i, e: (i, 0))
    return pl.pallas_call(
        functools.partial(_moe_kernel, alpha=alpha, n_steps=n_steps),
        out_shape=jax.ShapeDtypeStruct((t, d), F32),
        grid=(t // tb, n_steps),
        in_specs=[tok,
                  pl.BlockSpec((tb, N_EXPERTS), lambda i, e: (i, 0)),
                  tok,
                  pl.BlockSpec((1, 6, d), lambda i, e: (i // (seq // tb), 0, 0)),
                  pl.BlockSpec((eps, d, D_EXPERT), lambda i, e: (e, 0, 0)),
                  pl.BlockSpec((eps, d, D_EXPERT), lambda i, e: (e, 0, 0)),
                  pl.BlockSpec((eps, D_EXPERT, d), lambda i, e: (e, 0, 0)),
                  resident(sh_gate.shape), resident(sh_up.shape), resident(sh_down.shape),
                  const((1, d)), const((1, d))],
        out_specs=tok,
        compiler_params=pltpu.CompilerParams(dimension_semantics=("parallel", "arbitrary"),
                                             vmem_limit_bytes=MOE_VMEM_LIMIT),
        name="moe",
    )(u2, gate, x1, mod, exp_gate.astype(BF16), exp_up.astype(BF16), exp_down.astype(BF16),
      sh_gate.astype(BF16), sh_up.astype(BF16), sh_down.astype(BF16), row(ln2_w), row(ln2_b))


def kernel(x, c, ada_w, ada_b, w_in, shift_mu, w0, w_up, a0, a_up, g_up, k_k, k_a, r_k, lnx_w, lnx_b,
           conv_w, conv_b, cnorm_w, cnorm_b, cpw_w, cpw_b, gate_b, w_o, ln1_w, ln1_b, router_w, router_b,
           exp_gate, exp_up, exp_down, sh_gate, sh_up, sh_down, ln2_w, ln2_b):
    batch, seq, d = x.shape
    depth = ada_w.shape[0]
    alpha = (2.0 * depth) ** 0.25
    x2 = x.reshape(batch * seq, d)
    for l in range(depth):
        mod = _adaln(c, ada_w[l], ada_b[l])
        p_rw, p_conv, p_gate = _in_proj(x2, mod, w_in[l], seq)
        r, k, v, kk, lw_f, lw_b, a_f, a_b, g, bonus = _rwkv_prep(
            p_rw, seq, shift_mu[l], w0[l], w_up[l], a0[l], a_up[l], g_up[l], k_k[l], k_a[l], r_k[l])
        y_f, y_b = _wkv_scan(r, k, v, kk, lw_f, lw_b, a_f, a_b, k_a[l], batch, seq)
        x1, u2_hi, u2_lo = _mix_post(y_f, y_b, g, bonus, p_conv, p_gate, x2, mod, seq, alpha, lnx_w[l],
                                     lnx_b[l], conv_w[l], conv_b[l], cnorm_w[l], cnorm_b[l], cpw_w[l],
                                     cpw_b[l], gate_b[l], w_o[l], ln1_w[l], ln1_b[l])
        gate = _router(u2_hi, u2_lo, router_w[l], router_b[l])
        x2 = _moe(u2_hi, gate, x1, mod, seq, alpha, exp_gate[l], exp_up[l], exp_down[l], sh_gate[l],
                  sh_up[l], sh_down[l], ln2_w[l], ln2_b[l])
    return x2.reshape(batch, seq, d)
```
